```python
import math
import jax, jax.numpy as jnp
from jax import lax
import numpy as np

D_MODEL = 1024
BATCH = 8
SEQ = 8192
DEPTH = 4

N_A_LAYERS = DEPTH // 2
N_B_LAYERS = DEPTH - N_A_LAYERS

DIFF_HEADS = 8
DIFF_SUB_DIM = D_MODEL // (2 * DIFF_HEADS)
DIFF_V_DIM = 2 * DIFF_SUB_DIM

SWA_Q_HEADS = 16
SWA_KV_HEADS = 2
SWA_HEAD_DIM = D_MODEL // SWA_Q_HEADS
SWA_GROUP = SWA_Q_HEADS // SWA_KV_HEADS
SWA_KV_WIDTH = SWA_KV_HEADS * SWA_HEAD_DIM
WINDOW = 128
BLOCK = 128

D_FF = 2816

ROPE_THETA = 500000.0
ROT_DIM = SWA_HEAD_DIM // 4

NORM_EPS = 1e-5

kernel_name = "yoco_diffattn_swa_sink_macaron"


def rms_norm(x, g):
    xf = x.astype(jnp.float32)
    y = xf * lax.rsqrt(jnp.mean(xf * xf, axis=-1, keepdims=True) + NORM_EPS)
    return (y * g.astype(jnp.float32)).astype(x.dtype)


def rope_tables(positions):
    inv_freq = ROPE_THETA ** (-jnp.arange(0, ROT_DIM, 2, dtype=jnp.float32) / ROT_DIM)
    ang = positions.astype(jnp.float32)[..., None] * inv_freq
    return jnp.cos(ang), jnp.sin(ang)


def apply_partial_rope(x, cos, sin):
    half = cos.shape[-1]
    rot = 2 * half
    bshape = cos.shape[:2] + (1,) * (x.ndim - 3) + (half,)
    c = cos.reshape(bshape).astype(x.dtype)
    s = sin.reshape(bshape).astype(x.dtype)
    x1 = x[..., :half]
    x2 = x[..., half:rot]
    return jnp.concatenate([x1 * c - x2 * s, x2 * c + x1 * s, x[..., rot:]], axis=-1)


def swiglu(x, w_gate, w_up, w_down):
    return (jax.nn.silu(x @ w_gate) * (x @ w_up)) @ w_down


def diff_attention(u, w_qkv, w_o, lq1, lk1, lq2, lk2, g_sub, cos, sin, lambda_init):
    B, S, _ = u.shape
    nblk = S // BLOCK
    qkv = u @ w_qkv
    q, k, v = jnp.split(qkv, [D_MODEL, 2 * D_MODEL], axis=-1)
    q = apply_partial_rope(q.reshape(B, S, DIFF_HEADS, 2, DIFF_SUB_DIM), cos, sin)
    k = apply_partial_rope(k.reshape(B, S, DIFF_HEADS, 2, DIFF_SUB_DIM), cos, sin)
    v = v.reshape(B, S, DIFF_HEADS, DIFF_V_DIM)
    f32 = jnp.float32
    lam = (jnp.exp(jnp.sum(lq1.astype(f32) * lk1.astype(f32)))
           - jnp.exp(jnp.sum(lq2.astype(f32) * lk2.astype(f32))) + lambda_init)
    scale = DIFF_SUB_DIM ** -0.5
    q_blocks = q.reshape(B, nblk, BLOCK, DIFF_HEADS, 2, DIFF_SUB_DIM).transpose(1, 0, 2, 3, 4, 5)
    key_idx = jnp.arange(S)

    def one_block(args):
        q_blk, blk = args
        s = jnp.einsum('bqhcd,bkhcd->bhcqk', q_blk, k).astype(f32) * scale
        q_idx = blk * BLOCK + jnp.arange(BLOCK)
        causal = key_idx[None, :] <= q_idx[:, None]
        p = jax.nn.softmax(jnp.where(causal, s, -jnp.inf), axis=-1)
        a = p[:, :, 0] - lam * p[:, :, 1]
        return jnp.einsum('bhqk,bkhe->bqhe', a.astype(v.dtype), v)

    o = lax.map(one_block, (q_blocks, jnp.arange(nblk)))
    o = o.transpose(1, 0, 2, 3, 4).reshape(B, S, DIFF_HEADS, DIFF_V_DIM)
    o = rms_norm(o, g_sub) * (1.0 - lambda_init)
    return o.reshape(B, S, D_MODEL) @ w_o


def banded(t):
    B, S = t.shape[:2]
    tb = t.reshape(B, S // BLOCK, BLOCK, SWA_KV_HEADS, SWA_HEAD_DIM)
    prev = jnp.concatenate([jnp.zeros_like(tb[:, :1]), tb[:, :-1]], axis=1)
    return jnp.concatenate([prev, tb], axis=2)


def shared_kv(h, g_kv, w_k, b_k, w_v, b_v, cos, sin):
    B, S, _ = h.shape
    u = rms_norm(h, g_kv)
    k = apply_partial_rope((u @ w_k + b_k).reshape(B, S, SWA_KV_HEADS, SWA_HEAD_DIM), cos, sin)
    v = (u @ w_v + b_v).reshape(B, S, SWA_KV_HEADS, SWA_HEAD_DIM)
    return banded(k), banded(v)


def band_mask(nblk):
    i = jnp.arange(BLOCK)[:, None]
    j = jnp.arange(2 * BLOCK)[None, :]
    dist = i + BLOCK - j
    in_win = (dist >= 0) & (dist < WINDOW)
    has_prev = (jnp.arange(nblk) > 0)[:, None, None]
    return in_win[None] & (has_prev | (j >= BLOCK)[None])


def swa_sink_attention(u, w_q, b_q, sinks, w_o, b_o, k_band, v_band, cos, sin):
    B, S, _ = u.shape
    nblk = S // BLOCK
    q = apply_partial_rope((u @ w_q + b_q).reshape(B, S, SWA_Q_HEADS, SWA_HEAD_DIM), cos, sin)
    q = q.reshape(B, nblk, BLOCK, SWA_KV_HEADS, SWA_GROUP, SWA_HEAD_DIM)
    f32 = jnp.float32
    s = jnp.einsum('bnqhgd,bnkhd->bnhgqk', q, k_band).astype(f32) * (SWA_HEAD_DIM ** -0.5)
    mask = band_mask(nblk)[None, :, None, None]
    s = jnp.where(mask, s, -jnp.inf)
    sink = sinks.astype(f32).reshape(1, 1, SWA_KV_HEADS, SWA_GROUP, 1, 1)
    m = jnp.maximum(jnp.max(s, axis=-1, keepdims=True), sink)
    p = jnp.exp(s - m)
    p = p / (jnp.sum(p, axis=-1, keepdims=True) + jnp.exp(sink - m))
    o = jnp.einsum('bnhgqk,bnkhd->bnqhgd', p.astype(v_band.dtype), v_band)
    return o.reshape(B, S, D_MODEL) @ w_o + b_o


def setup_inputs(seed: int = 0) -> dict:
    key = jax.random.key(seed)
    ks = iter(jax.random.split(key, 40))
    f32 = jnp.float32

    def dense(shape, fan_in):
        return jax.random.normal(next(ks), shape, f32) * (fan_in ** -0.5)

    def gain(shape):
        return 1.0 + 0.02 * jax.random.normal(next(ks), shape, f32)

    def small(shape, s):
        return s * jax.random.normal(next(ks), shape, f32)

    x = jax.random.normal(next(ks), (BATCH, SEQ, D_MODEL), f32)
    offsets = jax.random.randint(next(ks), (BATCH, 1), 0, 4096, dtype=jnp.int32)
    positions = jnp.arange(SEQ, dtype=jnp.int32)[None, :] + offsets
    return {
        "x": x,
        "positions": positions,
        "ln_ffn1": gain((DEPTH, D_MODEL)),
        "ffn1_w_gate": dense((DEPTH, D_MODEL, D_FF), D_MODEL),
        "ffn1_w_up": dense((DEPTH, D_MODEL, D_FF), D_MODEL),
        "ffn1_w_down": dense((DEPTH, D_FF, D_MODEL), D_FF),
        "ln_mix": gain((DEPTH, D_MODEL)),
        "ln_ffn2": gain((DEPTH, D_MODEL)),
        "ffn2_w_gate": dense((DEPTH, D_MODEL, D_FF), D_MODEL),
        "ffn2_w_up": dense((DEPTH, D_MODEL, D_FF), D_MODEL),
        "ffn2_w_down": dense((DEPTH, D_FF, D_MODEL), D_FF),
        "a_w_qkv": dense((N_A_LAYERS, D_MODEL, 3 * D_MODEL), D_MODEL),
        "a_w_o": dense((N_A_LAYERS, D_MODEL, D_MODEL), D_MODEL),
        "a_lambda_q1": small((N_A_LAYERS, DIFF_SUB_DIM), 0.1),
        "a_lambda_k1": small((N_A_LAYERS, DIFF_SUB_DIM), 0.1),
        "a_lambda_q2": small((N_A_LAYERS, DIFF_SUB_DIM), 0.1),
        "a_lambda_k2": small((N_A_LAYERS, DIFF_SUB_DIM), 0.1),
        "a_subln": gain((N_A_LAYERS, DIFF_V_DIM)),
        "b_w_q": dense((N_B_LAYERS, D_MODEL, D_MODEL), D_MODEL),
        "b_b_q": small((N_B_LAYERS, D_MODEL), 0.02),
        "b_sinks": small((N_B_LAYERS, SWA_Q_HEADS), 0.5),
        "b_w_o": dense((N_B_LAYERS, D_MODEL, D_MODEL), D_MODEL),
        "b_b_o": small((N_B_LAYERS, D_MODEL), 0.02),
        "kv_norm": gain((D_MODEL,)),
        "kv_w_k": dense((D_MODEL, SWA_KV_WIDTH), D_MODEL),
        "kv_b_k": small((SWA_KV_WIDTH,), 0.02),
        "kv_w_v": dense((D_MODEL, SWA_KV_WIDTH), D_MODEL),
        "kv_b_v": small((SWA_KV_WIDTH,), 0.02),
        "final_norm": gain((D_MODEL,)),
    }


def reference(x, positions, ln_ffn1, ffn1_w_gate, ffn1_w_up, ffn1_w_down, ln_mix, ln_ffn2,
              ffn2_w_gate, ffn2_w_up, ffn2_w_down, a_w_qkv, a_w_o, a_lambda_q1, a_lambda_k1,
              a_lambda_q2, a_lambda_k2, a_subln, b_w_q, b_b_q, b_sinks, b_w_o, b_b_o,
              kv_norm, kv_w_k, kv_b_k, kv_w_v, kv_b_v, final_norm):
    cos, sin = rope_tables(positions)
    h = x
    k_band = None
    v_band = None
    for layer in range(DEPTH):
        h = h + 0.5 * swiglu(rms_norm(h, ln_ffn1[layer]), ffn1_w_gate[layer],
                             ffn1_w_up[layer], ffn1_w_down[layer])
        u = rms_norm(h, ln_mix[layer])
        if layer < N_A_LAYERS:
            a = layer
            lambda_init = 0.8 - 0.6 * math.exp(-0.3 * layer)
            h = h + diff_attention(u, a_w_qkv[a], a_w_o[a], a_lambda_q1[a], a_lambda_k1[a],
                                   a_lambda_q2[a], a_lambda_k2[a], a_subln[a], cos, sin,
                                   lambda_init)
        else:
            b = layer - N_A_LAYERS
            h = h + swa_sink_attention(u, b_w_q[b], b_b_q[b], b_sinks[b], b_w_o[b], b_b_o[b],
                                       k_band, v_band, cos, sin)
        h = h + 0.5 * swiglu(rms_norm(h, ln_ffn2[layer]), ffn2_w_gate[layer],
                             ffn2_w_up[layer], ffn2_w_down[layer])
        if layer == N_A_LAYERS - 1:
            k_band, v_band = shared_kv(h, kv_norm, kv_w_k, kv_b_k, kv_w_v, kv_b_v, cos, sin)
    return rms_norm(h, final_norm)
```

```python
import functools
import math

import jax
import jax.numpy as jnp
from jax import lax
from jax.experimental import pallas as pl
from jax.experimental.pallas import tpu as pltpu

D_MODEL = 1024
D_FF = 2816
DEPTH = 4
N_A_LAYERS = 2
DIFF_HEADS = 8
DIFF_SUB_DIM = 64
DIFF_V_DIM = 128
SWA_Q_HEADS = 16
SWA_KV_HEADS = 2
SWA_HEAD_DIM = 64
SWA_GROUP = SWA_Q_HEADS // SWA_KV_HEADS
SWA_KV_WIDTH = SWA_KV_HEADS * SWA_HEAD_DIM
WINDOW = 128
BLOCK = 128
ROPE_THETA = 500000.0
ROT_DIM = 16
ROT_HALF = ROT_DIM // 2
NORM_EPS = 1e-5

LANES = 128
VMEM_LIMIT = 56 * 1024 * 1024
MASK_VALUE = -1e30

F32 = jnp.float32
BF16 = jnp.bfloat16


def _params(*sem):
    return pltpu.CompilerParams(dimension_semantics=sem, vmem_limit_bytes=VMEM_LIMIT)


def _resident(shape):
    return pl.BlockSpec(shape, lambda *_: (0,) * len(shape), pipeline_mode=pl.Buffered(1))


def _rms_norm(x, g):
    ms = jnp.mean(x * x, axis=-1, keepdims=True)
    return x * lax.rsqrt(ms + NORM_EPS) * g


def _rope_table_kernel(pos_ref, invf_ref, c_ref, s1_ref, s2_ref):
    ang = pos_ref[...] * invf_ref[...]
    cos = jnp.cos(ang)
    sin = jnp.sin(ang)
    lane = lax.broadcasted_iota(jnp.int32, ang.shape, 1) % SWA_HEAD_DIM
    first = lane < ROT_HALF
    second = (lane >= ROT_HALF) & (lane < ROT_DIM)
    c_ref[...] = jnp.where(first | second, cos, 1.0)
    s1_ref[...] = jnp.where(first, -sin, 0.0)
    s2_ref[...] = jnp.where(second, sin, 0.0)


def _rope_tables(positions, tm=1024):
    n = positions.size
    pos = jnp.broadcast_to(positions.reshape(n, 1).astype(F32), (n, LANES))
    inv_freq = ROPE_THETA ** (-jnp.arange(0, ROT_DIM, 2, dtype=F32) / ROT_DIM)
    invf = jnp.tile(inv_freq, LANES // ROT_HALF).reshape(1, LANES)
    spec = pl.BlockSpec((tm, LANES), lambda i: (i, 0))
    out = jax.ShapeDtypeStruct((n, LANES), F32)
    return pl.pallas_call(
        _rope_table_kernel,
        out_shape=(out, out, out),
        grid=(n // tm,),
        in_specs=[spec, pl.BlockSpec((1, LANES), lambda i: (0, 0))],
        out_specs=(spec, spec, spec),
        compiler_params=_params("arbitrary"),
        name="rope_tables",
    )(pos, invf)


def _apply_rope(x, c, s1, s2):
    return (x * c + pltpu.roll(x, LANES - ROT_HALF, 1) * s1 + pltpu.roll(x, ROT_HALF, 1) * s2)


def _ffn_kernel(h_ref, g_ref, wg_ref, wu_ref, wd_ref, o_ref, xn_ref, acc_ref, *, tf):
    xn_ref[...] = _rms_norm(h_ref[...], g_ref[...]).astype(BF16)
    for c in range(D_FF // tf):
        cols = slice(c * tf, (c + 1) * tf)
        xn = xn_ref[...]
        gate = jnp.dot(xn, wg_ref[:, cols], preferred_element_type=F32)
        up = jnp.dot(xn, wu_ref[:, cols], preferred_element_type=F32)
        act = (gate / (1.0 + jnp.exp(-gate)) * up).astype(BF16)
        down = jnp.dot(act, wd_ref[cols, :], preferred_element_type=F32)
        if c == 0:
            acc_ref[...] = down
        else:
            acc_ref[...] += down
    o_ref[...] = h_ref[...] + 0.5 * acc_ref[...]


def _ffn(h, g, wg, wu, wd, *, tm=512, tf=256):
    n = h.shape[0]
    row = pl.BlockSpec((tm, D_MODEL), lambda i: (i, 0))
    return pl.pallas_call(
        functools.partial(_ffn_kernel, tf=tf),
        out_shape=jax.ShapeDtypeStruct((n, D_MODEL), F32),
        grid=(n // tm,),
        in_specs=[row, _resident((1, D_MODEL)), _resident((D_MODEL, D_FF)),
                  _resident((D_MODEL, D_FF)), _resident((D_FF, D_MODEL))],
        out_specs=row,
        scratch_shapes=[pltpu.VMEM((tm, D_MODEL), BF16), pltpu.VMEM((tm, D_MODEL), F32)],
        compiler_params=_params("arbitrary"),
        name="ffn",
    )(h, g.reshape(1, D_MODEL), wg, wu, wd)


def _proj_kernel(h_ref, g_ref, w_ref, b_ref, c_ref, s1_ref, s2_ref, o_ref, xn_ref, *,
                 n_out, rope_cols, scale_cols, scale, chunk):
    xn_ref[...] = _rms_norm(h_ref[...], g_ref[...]).astype(BF16)
    c, s1, s2 = c_ref[...], s1_ref[...], s2_ref[...]
    for c0 in range(0, n_out, chunk):
        cols = slice(c0, c0 + chunk)
        y = jnp.dot(xn_ref[...], w_ref[:, cols], preferred_element_type=F32) + b_ref[:, cols]
        for j in range(0, chunk, LANES):
            yj = y[:, j:j + LANES]
            if c0 + j < rope_cols:
                yj = _apply_rope(yj, c, s1, s2)
            if c0 + j < scale_cols:
                yj = yj * scale
            o_ref[:, c0 + j:c0 + j + LANES] = yj.astype(BF16)


def _proj(h, g, w, b, tables, *, rope_cols, scale_cols, scale, tm=512):
    n = h.shape[0]
    n_out = w.shape[1]
    chunk = min(n_out, 512)
    assert rope_cols % LANES == 0 and scale_cols % LANES == 0 and n_out % chunk == 0
    row = pl.BlockSpec((tm, D_MODEL), lambda i: (i, 0))
    tab = pl.BlockSpec((tm, LANES), lambda i: (i, 0))
    return pl.pallas_call(
        functools.partial(_proj_kernel, n_out=n_out, rope_cols=rope_cols, scale_cols=scale_cols,
                          scale=scale, chunk=chunk),
        out_shape=jax.ShapeDtypeStruct((n, n_out), BF16),
        grid=(n // tm,),
        in_specs=[row, _resident((1, D_MODEL)), _resident((D_MODEL, n_out)), _resident((1, n_out)),
                  tab, tab, tab],
        out_specs=pl.BlockSpec((tm, n_out), lambda i: (i, 0)),
        scratch_shapes=[pltpu.VMEM((tm, D_MODEL), BF16)],
        compiler_params=_params("arbitrary"),
        name="norm_proj",
    )(h, g.reshape(1, D_MODEL), w, b.reshape(1, n_out), *tables)


def _out_proj_kernel(h_ref, o_ref, w_ref, b_ref, out_ref):
    out_ref[...] = (h_ref[...] + b_ref[...]
                    + jnp.dot(o_ref[...], w_ref[...], preferred_element_type=F32))


def _out_proj(h, o, w, b, *, tm=512):
    n = h.shape[0]
    row = pl.BlockSpec((tm, D_MODEL), lambda i: (i, 0))
    return pl.pallas_call(
        _out_proj_kernel,
        out_shape=jax.ShapeDtypeStruct((n, D_MODEL), F32),
        grid=(n // tm,),
        in_specs=[row, row, _resident((D_MODEL, D_MODEL)), _resident((1, D_MODEL))],
        out_specs=row,
        compiler_params=_params("arbitrary"),
        name="out_proj",
    )(h, o, w, b.reshape(1, D_MODEL))


def _diff_attn_kernel(lam_ref, gsub_ref, q_ref, k_ref, v_ref, o_ref, m_ref, l_ref, acc_ref, *,
                      t, lambda_init):
    i = pl.program_id(2)
    q = q_ref[0]
    lane = lax.broadcasted_iota(jnp.int32, q.shape, 1)
    zero = jnp.zeros_like(q)
    q2 = jnp.concatenate([jnp.where(lane < DIFF_SUB_DIM, q, zero),
                          jnp.where(lane >= DIFF_SUB_DIM, q, zero)], axis=0)
    m_ref[...] = jnp.full(m_ref.shape, MASK_VALUE, F32)
    l_ref[...] = jnp.zeros(l_ref.shape, F32)
    acc_ref[...] = jnp.zeros(acc_ref.shape, F32)

    def step(j, masked):
        k = k_ref[0, pl.ds(j * t, t), :]
        v = v_ref[0, pl.ds(j * t, t), :]
        s = lax.dot_general(q2, k, (((1,), (1,)), ((), ())), preferred_element_type=F32)
        if masked:
            row = lax.broadcasted_iota(jnp.int32, s.shape, 0) % t
            col = lax.broadcasted_iota(jnp.int32, s.shape, 1)
            s = jnp.where(col <= row, s, MASK_VALUE)
        m_prev = m_ref[...]
        m_next = jnp.maximum(m_prev, jnp.max(s, axis=-1, keepdims=True))
        alpha = jnp.exp(m_prev - m_next)
        p = jnp.exp(s - jnp.tile(m_next, (1, t // LANES)))
        l_ref[...] = alpha * l_ref[...] + jnp.sum(p, axis=-1, keepdims=True)
        acc_ref[...] = alpha * acc_ref[...] + jnp.dot(p.astype(BF16), v, preferred_element_type=F32)
        m_ref[...] = m_next

    def body(j, carry):
        step(j, False)
        return carry

    lax.fori_loop(0, i, body, 0)
    step(i, True)

    lv = lam_ref[...]
    lam = (jnp.exp(jnp.sum(lv[0:1] * lv[1:2], axis=-1, keepdims=True))
           - jnp.exp(jnp.sum(lv[2:3] * lv[3:4], axis=-1, keepdims=True)) + lambda_init)
    attn = acc_ref[...] / l_ref[...]
    o = attn[:t] - lam * attn[t:]
    o = _rms_norm(o, gsub_ref[...]) * (1.0 - lambda_init)
    o_ref[0] = o.astype(BF16)


def _diff_attention(qkv, lam_vecs, g_sub, lambda_init, *, t=256):
    b, s, _ = qkv.shape
    h = DIFF_HEADS
    return pl.pallas_call(
        functools.partial(_diff_attn_kernel, t=t, lambda_init=lambda_init),
        out_shape=jax.ShapeDtypeStruct((b, s, D_MODEL), BF16),
        grid=(b, h, s // t),
        in_specs=[
            pl.BlockSpec((4, DIFF_SUB_DIM), lambda bi, hi, i: (0, 0)),
            pl.BlockSpec((1, DIFF_V_DIM), lambda bi, hi, i: (0, 0)),
            pl.BlockSpec((1, t, DIFF_V_DIM), lambda bi, hi, i: (bi, i, hi)),
            pl.BlockSpec((1, s, DIFF_V_DIM), lambda bi, hi, i: (bi, 0, h + hi)),
            pl.BlockSpec((1, s, DIFF_V_DIM), lambda bi, hi, i: (bi, 0, 2 * h + hi)),
        ],
        out_specs=pl.BlockSpec((1, t, DIFF_V_DIM), lambda bi, hi, i: (bi, i, hi)),
        scratch_shapes=[pltpu.VMEM((2 * t, LANES), F32), pltpu.VMEM((2 * t, LANES), F32),
                        pltpu.VMEM((2 * t, DIFF_V_DIM), F32)],
        compiler_params=_params("arbitrary", "arbitrary", "arbitrary"),
        name="diff_attention",
    )(lam_vecs, g_sub.reshape(1, DIFF_V_DIM), qkv, qkv, qkv)


def _swa_kernel(sink_ref, q_ref, kp_ref, kc_ref, vp_ref, vc_ref, o_ref, *, tq):
    i = pl.program_id(1)
    kk = jnp.concatenate([kp_ref[0], kc_ref[0]], axis=0)
    vv = jnp.concatenate([vp_ref[0], vc_ref[0]], axis=0)
    nj = D_MODEL // LANES
    rows = 2 * nj * BLOCK
    lane_q = lax.broadcasted_iota(jnp.int32, (BLOCK, LANES), 1)
    qi = lax.broadcasted_iota(jnp.int32, (rows, 2 * BLOCK), 0) % BLOCK
    kj = lax.broadcasted_iota(jnp.int32, (rows, 2 * BLOCK), 1)
    in_window = (kj > qi) & (kj <= qi + WINDOW)
    sink = sink_ref[...]
    for c in range(tq // BLOCK):
        q = q_ref[0, c * BLOCK:(c + 1) * BLOCK, :]
        parts = []
        for j in range(nj):
            qj = q[:, j * LANES:(j + 1) * LANES]
            parts.append(jnp.where(lane_q < SWA_HEAD_DIM, qj, jnp.zeros_like(qj)))
            parts.append(jnp.where(lane_q >= SWA_HEAD_DIM, qj, jnp.zeros_like(qj)))
        q2 = jnp.concatenate(parts, axis=0)
        k = kk[c * BLOCK:(c + 2) * BLOCK]
        v = vv[c * BLOCK:(c + 2) * BLOCK]
        s = lax.dot_general(q2, k, (((1,), (1,)), ((), ())), preferred_element_type=F32)
        allowed = in_window
        if c == 0:
            allowed = allowed & (kj >= jnp.where(i > 0, 0, BLOCK))
        s = jnp.where(allowed, s, MASK_VALUE)
        m = jnp.maximum(jnp.max(s, axis=-1, keepdims=True), sink)
        p = jnp.exp(s - m)
        denom = jnp.sum(p, axis=-1, keepdims=True) + jnp.exp(sink - m)
        o2 = jnp.dot(p.astype(BF16), v, preferred_element_type=F32) / denom
        outs = []
        for j in range(nj):
            lo = o2[(2 * j) * BLOCK:(2 * j + 1) * BLOCK]
            hi = o2[(2 * j + 1) * BLOCK:(2 * j + 2) * BLOCK]
            outs.append(jnp.where(lane_q < SWA_HEAD_DIM, lo, hi))
        o_ref[0, c * BLOCK:(c + 1) * BLOCK, :] = jnp.concatenate(outs, axis=1).astype(BF16)


def _swa_attention(q, kv, sink_rows, *, tq=512):
    b, s, _ = q.shape
    per = tq // BLOCK
    cur = lambda bi, i: (bi, i, 0)
    prev = lambda bi, i: (bi, jnp.maximum(i * per - 1, 0), 0)
    cur_v = lambda bi, i: (bi, i, 1)
    prev_v = lambda bi, i: (bi, jnp.maximum(i * per - 1, 0), 1)
    rows = sink_rows.shape[0]
    return pl.pallas_call(
        functools.partial(_swa_kernel, tq=tq),
        out_shape=jax.ShapeDtypeStruct((b, s, D_MODEL), BF16),
        grid=(b, s // tq),
        in_specs=[
            pl.BlockSpec((rows, 1), lambda bi, i: (0, 0)),
            pl.BlockSpec((1, tq, D_MODEL), cur),
            pl.BlockSpec((1, BLOCK, LANES), prev),
            pl.BlockSpec((1, tq, LANES), cur),
            pl.BlockSpec((1, BLOCK, LANES), prev_v),
            pl.BlockSpec((1, tq, LANES), cur_v),
        ],
        out_specs=pl.BlockSpec((1, tq, D_MODEL), cur),
        compiler_params=_params("arbitrary", "arbitrary"),
        name="swa_attention",
    )(sink_rows, q, kv, kv, kv, kv)


def _final_norm_kernel(h_ref, g_ref, o_ref):
    o_ref[...] = _rms_norm(h_ref[...], g_ref[...])


def _final_norm(h, g, *, tm=1024):
    n = h.shape[0]
    row = pl.BlockSpec((tm, D_MODEL), lambda i: (i, 0))
    return pl.pallas_call(
        _final_norm_kernel,
        out_shape=jax.ShapeDtypeStruct((n, D_MODEL), F32),
        grid=(n // tm,),
        in_specs=[row, pl.BlockSpec((1, D_MODEL), lambda i: (0, 0))],
        out_specs=row,
        compiler_params=_params("arbitrary"),
        name="final_norm",
    )(h, g.reshape(1, D_MODEL))


def _paired_head_order():
    cols = []
    for j in range(SWA_GROUP):
        cols.extend(range(j * SWA_HEAD_DIM, (j + 1) * SWA_HEAD_DIM))
        cols.extend(range((SWA_GROUP + j) * SWA_HEAD_DIM, (SWA_GROUP + j + 1) * SWA_HEAD_DIM))
    return jnp.asarray(cols, dtype=jnp.int32)


def kernel(x, positions, ln_ffn1, ffn1_w_gate, ffn1_w_up, ffn1_w_down, ln_mix, ln_ffn2, ffn2_w_gate, ffn2_w_up, ffn2_w_down, a_w_qkv, a_w_o, a_lambda_q1, a_lambda_k1, a_lambda_q2, a_lambda_k2, a_subln, b_w_q, b_b_q, b_sinks, b_w_o, b_b_o, kv_norm, kv_w_k, kv_b_k, kv_w_v, kv_b_v, final_norm):
    batch, seq, _ = x.shape
    n = batch * seq
    tables = _rope_tables(positions)
    h = x.reshape(n, D_MODEL)
    perm = _paired_head_order()
    kv = None
    for layer in range(DEPTH):
        h = _ffn(h, ln_ffn1[layer], ffn1_w_gate[layer].astype(BF16), ffn1_w_up[layer].astype(BF16),
                 ffn1_w_down[layer].astype(BF16))
        if layer < N_A_LAYERS:
            a = layer
            lambda_init = 0.8 - 0.6 * math.exp(-0.3 * layer)
            qkv = _proj(h, ln_mix[layer], a_w_qkv[a].astype(BF16), jnp.zeros((3 * D_MODEL,), F32),
                        tables, rope_cols=2 * D_MODEL, scale_cols=D_MODEL,
                        scale=DIFF_SUB_DIM ** -0.5)
            lam_vecs = jnp.stack([a_lambda_q1[a], a_lambda_k1[a], a_lambda_q2[a], a_lambda_k2[a]])
            o = _diff_attention(qkv.reshape(batch, seq, 3 * D_MODEL), lam_vecs, a_subln[a],
                                lambda_init)
            h = _out_proj(h, o.reshape(n, D_MODEL), a_w_o[a].astype(BF16),
                          jnp.zeros((D_MODEL,), F32))
        else:
            b = layer - N_A_LAYERS
            q = _proj(h, ln_mix[layer], b_w_q[b][:, perm].astype(BF16), b_b_q[b][perm], tables,
                      rope_cols=D_MODEL, scale_cols=D_MODEL, scale=SWA_HEAD_DIM ** -0.5)
            sink_pairs = jnp.stack([b_sinks[b][:SWA_GROUP], b_sinks[b][SWA_GROUP:]], axis=1)
            sink_rows = jnp.repeat(sink_pairs.reshape(-1), BLOCK).reshape(-1, 1)
            o = _swa_attention(q.reshape(batch, seq, D_MODEL), kv, sink_rows)
            h = _out_proj(h, o.reshape(n, D_MODEL), b_w_o[b][perm, :].astype(BF16), b_b_o[b])
        h = _ffn(h, ln_ffn2[layer], ffn2_w_gate[layer].astype(BF16), ffn2_w_up[layer].astype(BF16),
                 ffn2_w_down[layer].astype(BF16))
        if layer == N_A_LAYERS - 1:
            w_kv = jnp.concatenate([kv_w_k, kv_w_v], axis=1).astype(BF16)
            b_kv = jnp.concatenate([kv_b_k, kv_b_v])
            kv = _proj(h, kv_norm, w_kv, b_kv, tables, rope_cols=SWA_KV_WIDTH, scale_cols=0,
                       scale=1.0).reshape(batch, seq, 2 * SWA_KV_WIDTH)
    return _final_norm(h, final_norm).reshape(batch, seq, D_MODEL)
```

```python
import functools
import math

import jax
import jax.numpy as jnp
from jax import lax
from jax.experimental import pallas as pl
from jax.experimental.pallas import tpu as pltpu

D_MODEL = 1024
D_FF = 2816
DEPTH = 4
N_A_LAYERS = 2
DIFF_HEADS = 8
DIFF_SUB_DIM = 64
DIFF_V_DIM = 128
SWA_Q_HEADS = 16
SWA_KV_HEADS = 2
SWA_HEAD_DIM = 64
SWA_GROUP = SWA_Q_HEADS // SWA_KV_HEADS
SWA_KV_WIDTH = SWA_KV_HEADS * SWA_HEAD_DIM
WINDOW = 128
BLOCK = 128
ROPE_THETA = 500000.0
ROT_DIM = 16
ROT_HALF = ROT_DIM // 2
NORM_EPS = 1e-5

LANES = 128
VMEM_LIMIT = 56 * 1024 * 1024
MASK_VALUE = -1e30
LOG2_E = math.log2(math.e)

F32 = jnp.float32
BF16 = jnp.bfloat16


def _params(*sem):
    return pltpu.CompilerParams(dimension_semantics=sem, vmem_limit_bytes=VMEM_LIMIT)


def _resident(shape):
    return pl.BlockSpec(shape, lambda *_: (0,) * len(shape), pipeline_mode=pl.Buffered(1))


def _rms_norm(x, g):
    ms = jnp.mean(x * x, axis=-1, keepdims=True)
    return x * lax.rsqrt(ms + NORM_EPS) * g


def _rope_table_kernel(pos_ref, invf_ref, c_ref, s1_ref, s2_ref):
    ang = pos_ref[...] * invf_ref[...]
    cos = jnp.cos(ang)
    sin = jnp.sin(ang)
    lane = lax.broadcasted_iota(jnp.int32, ang.shape, 1) % SWA_HEAD_DIM
    first = lane < ROT_HALF
    second = (lane >= ROT_HALF) & (lane < ROT_DIM)
    c_ref[...] = jnp.where(first | second, cos, 1.0)
    s1_ref[...] = jnp.where(first, -sin, 0.0)
    s2_ref[...] = jnp.where(second, sin, 0.0)


def _rope_tables(positions, tm=1024):
    n = positions.size
    pos = jnp.broadcast_to(positions.reshape(n, 1).astype(F32), (n, LANES))
    inv_freq = ROPE_THETA ** (-jnp.arange(0, ROT_DIM, 2, dtype=F32) / ROT_DIM)
    invf = jnp.tile(inv_freq, LANES // ROT_HALF).reshape(1, LANES)
    spec = pl.BlockSpec((tm, LANES), lambda i: (i, 0))
    out = jax.ShapeDtypeStruct((n, LANES), F32)
    return pl.pallas_call(
        _rope_table_kernel,
        out_shape=(out, out, out),
        grid=(n // tm,),
        in_specs=[spec, pl.BlockSpec((1, LANES), lambda i: (0, 0))],
        out_specs=(spec, spec, spec),
        compiler_params=_params("arbitrary"),
        name="rope_tables",
    )(pos, invf)


def _apply_rope(x, c, s1, s2):
    return (x * c + pltpu.roll(x, LANES - ROT_HALF, 1) * s1 + pltpu.roll(x, ROT_HALF, 1) * s2)


def _ffn_kernel(h_ref, g_ref, wg_ref, wu_ref, wd_ref, o_ref, xn_ref, acc_ref, *, tf):
    xn_ref[...] = _rms_norm(h_ref[...], g_ref[...]).astype(BF16)
    for c in range(D_FF // tf):
        cols = slice(c * tf, (c + 1) * tf)
        xn = xn_ref[...]
        gate = jnp.dot(xn, wg_ref[:, cols], preferred_element_type=F32)
        up = jnp.dot(xn, wu_ref[:, cols], preferred_element_type=F32)
        act = (gate / (1.0 + jnp.exp(-gate)) * up).astype(BF16)
        down = jnp.dot(act, wd_ref[cols, :], preferred_element_type=F32)
        if c == 0:
            acc_ref[...] = down
        else:
            acc_ref[...] += down
    o_ref[...] = h_ref[...] + 0.5 * acc_ref[...]


def _ffn(h, g, wg, wu, wd, *, tm=512, tf=256):
    n = h.shape[0]
    row = pl.BlockSpec((tm, D_MODEL), lambda i: (i, 0))
    return pl.pallas_call(
        functools.partial(_ffn_kernel, tf=tf),
        out_shape=jax.ShapeDtypeStruct((n, D_MODEL), F32),
        grid=(n // tm,),
        in_specs=[row, _resident((1, D_MODEL)), _resident((D_MODEL, D_FF)),
                  _resident((D_MODEL, D_FF)), _resident((D_FF, D_MODEL))],
        out_specs=row,
        scratch_shapes=[pltpu.VMEM((tm, D_MODEL), BF16), pltpu.VMEM((tm, D_MODEL), F32)],
        compiler_params=_params("arbitrary"),
        name="ffn",
    )(h, g.reshape(1, D_MODEL), wg, wu, wd)


def _proj_kernel(h_ref, g_ref, w_ref, b_ref, c_ref, s1_ref, s2_ref, o_ref, xn_ref, *,
                 n_out, rope_cols, scale_cols, scale, chunk):
    xn_ref[...] = _rms_norm(h_ref[...], g_ref[...]).astype(BF16)
    c, s1, s2 = c_ref[...], s1_ref[...], s2_ref[...]
    for c0 in range(0, n_out, chunk):
        cols = slice(c0, c0 + chunk)
        y = jnp.dot(xn_ref[...], w_ref[:, cols], preferred_element_type=F32) + b_ref[:, cols]
        for j in range(0, chunk, LANES):
            yj = y[:, j:j + LANES]
            if c0 + j < rope_cols:
                yj = _apply_rope(yj, c, s1, s2)
            if c0 + j < scale_cols:
                yj = yj * scale
            o_ref[:, c0 + j:c0 + j + LANES] = yj.astype(BF16)


def _proj(h, g, w, b, tables, *, rope_cols, scale_cols, scale, tm=512):
    n = h.shape[0]
    n_out = w.shape[1]
    chunk = min(n_out, 512)
    assert rope_cols % LANES == 0 and scale_cols % LANES == 0 and n_out % chunk == 0
    row = pl.BlockSpec((tm, D_MODEL), lambda i: (i, 0))
    tab = pl.BlockSpec((tm, LANES), lambda i: (i, 0))
    return pl.pallas_call(
        functools.partial(_proj_kernel, n_out=n_out, rope_cols=rope_cols, scale_cols=scale_cols,
                          scale=scale, chunk=chunk),
        out_shape=jax.ShapeDtypeStruct((n, n_out), BF16),
        grid=(n // tm,),
        in_specs=[row, _resident((1, D_MODEL)), _resident((D_MODEL, n_out)), _resident((1, n_out)),
                  tab, tab, tab],
        out_specs=pl.BlockSpec((tm, n_out), lambda i: (i, 0)),
        scratch_shapes=[pltpu.VMEM((tm, D_MODEL), BF16)],
        compiler_params=_params("arbitrary"),
        name="norm_proj",
    )(h, g.reshape(1, D_MODEL), w, b.reshape(1, n_out), *tables)


def _out_proj_kernel(h_ref, o_ref, w_ref, b_ref, out_ref):
    out_ref[...] = (h_ref[...] + b_ref[...]
                    + jnp.dot(o_ref[...], w_ref[...], preferred_element_type=F32))


def _out_proj(h, o, w, b, *, tm=512):
    n = h.shape[0]
    row = pl.BlockSpec((tm, D_MODEL), lambda i: (i, 0))
    return pl.pallas_call(
        _out_proj_kernel,
        out_shape=jax.ShapeDtypeStruct((n, D_MODEL), F32),
        grid=(n // tm,),
        in_specs=[row, row, _resident((D_MODEL, D_MODEL)), _resident((1, D_MODEL))],
        out_specs=row,
        compiler_params=_params("arbitrary"),
        name="out_proj",
    )(h, o, w, b.reshape(1, D_MODEL))


def _diff_attn_kernel(lam_ref, gsub_ref, q_ref, k_ref, v_ref, o_ref, s_ref, p_ref, m_ref, acc_ref, *,
                      t, lambda_init):
    i = pl.program_id(2)
    q = q_ref[0]
    lane = lax.broadcasted_iota(jnp.int32, q.shape, 1)
    zero = jnp.zeros_like(q)
    q2 = jnp.concatenate([jnp.where(lane < DIFF_SUB_DIM, q, zero),
                          jnp.where(lane >= DIFF_SUB_DIM, q, zero)], axis=0)
    ones = jnp.ones((t, LANES), BF16)

    def scores(j):
        k = k_ref[0, pl.ds(j * t, t), :]
        return lax.dot_general(q2, k, (((1,), (1,)), ((), ())), preferred_element_type=F32)

    def weighted_values(j):
        v_aug = jnp.concatenate([v_ref[0, pl.ds(j * t, t), :], ones], axis=1)
        return jnp.dot(p_ref[...], v_aug, preferred_element_type=F32)

    def softmax_tile(masked):
        s = s_ref[...]
        if masked:
            row = lax.broadcasted_iota(jnp.int32, s.shape, 0) % t
            col = lax.broadcasted_iota(jnp.int32, s.shape, 1)
            s = jnp.where(col <= row, s, MASK_VALUE)
        m_prev = m_ref[...]
        m_next = jnp.maximum(m_prev, jnp.max(s, axis=-1, keepdims=True))
        m_ref[...] = m_next
        alpha = jnp.exp2(m_prev - m_next)
        p = jnp.exp2(s - jnp.tile(m_next, (1, t // LANES)))
        return jnp.tile(alpha, (1, 2)), p.astype(BF16)

    m_ref[...] = jnp.full(m_ref.shape, MASK_VALUE, F32)
    acc_ref[...] = jnp.zeros(acc_ref.shape, F32)
    p_ref[...] = jnp.zeros(p_ref.shape, BF16)
    s_ref[...] = scores(0)

    def body(j, carry):
        pv_prev = weighted_values(jnp.maximum(j - 1, 0))
        alpha, p = softmax_tile(False)
        s_ref[...] = scores(j + 1)
        p_ref[...] = p
        acc_ref[...] = alpha * (acc_ref[...] + pv_prev)
        return carry

    lax.fori_loop(0, i, body, 0)
    pv_prev = weighted_values(jnp.maximum(i - 1, 0))
    alpha, p = softmax_tile(True)
    p_ref[...] = p
    acc = alpha * (acc_ref[...] + pv_prev) + weighted_values(i)

    lv = lam_ref[...]
    lam = (jnp.exp(jnp.sum(lv[0:1] * lv[1:2], axis=-1, keepdims=True))
           - jnp.exp(jnp.sum(lv[2:3] * lv[3:4], axis=-1, keepdims=True)) + lambda_init)
    attn = acc[:, :DIFF_V_DIM] / acc[:, DIFF_V_DIM:]
    o = attn[:t] - lam * attn[t:]
    o = _rms_norm(o, gsub_ref[...]) * (1.0 - lambda_init)
    o_ref[0] = o.astype(BF16)


def _diff_attention(qkv, lam_vecs, g_sub, lambda_init, *, t=256):
    b, s, _ = qkv.shape
    h = DIFF_HEADS
    return pl.pallas_call(
        functools.partial(_diff_attn_kernel, t=t, lambda_init=lambda_init),
        out_shape=jax.ShapeDtypeStruct((b, s, D_MODEL), BF16),
        grid=(b, h, s // t),
        in_specs=[
            pl.BlockSpec((4, DIFF_SUB_DIM), lambda bi, hi, i: (0, 0)),
            pl.BlockSpec((1, DIFF_V_DIM), lambda bi, hi, i: (0, 0)),
            pl.BlockSpec((1, t, DIFF_V_DIM), lambda bi, hi, i: (bi, i, hi)),
            pl.BlockSpec((1, s, DIFF_V_DIM), lambda bi, hi, i: (bi, 0, h + hi)),
            pl.BlockSpec((1, s, DIFF_V_DIM), lambda bi, hi, i: (bi, 0, 2 * h + hi)),
        ],
        out_specs=pl.BlockSpec((1, t, DIFF_V_DIM), lambda bi, hi, i: (bi, i, hi)),
        scratch_shapes=[pltpu.VMEM((2 * t, t), F32), pltpu.VMEM((2 * t, t), BF16),
                        pltpu.VMEM((2 * t, LANES), F32), pltpu.VMEM((2 * t, 2 * DIFF_V_DIM), F32)],
        compiler_params=_params("arbitrary", "arbitrary", "arbitrary"),
        name="diff_attention",
    )(lam_vecs, g_sub.reshape(1, DIFF_V_DIM), qkv, qkv, qkv)


def _swa_kernel(sink_ref, q_ref, kp_ref, kc_ref, vp_ref, vc_ref, o_ref, *, tq):
    i = pl.program_id(1)
    kk = jnp.concatenate([kp_ref[0], kc_ref[0]], axis=0)
    vv = jnp.concatenate([vp_ref[0], vc_ref[0]], axis=0)
    nj = D_MODEL // LANES
    rows = 2 * nj * BLOCK
    lane_q = lax.broadcasted_iota(jnp.int32, (BLOCK, LANES), 1)
    qi = lax.broadcasted_iota(jnp.int32, (rows, 2 * BLOCK), 0) % BLOCK
    kj = lax.broadcasted_iota(jnp.int32, (rows, 2 * BLOCK), 1)
    in_window = (kj > qi) & (kj <= qi + WINDOW)
    sink = sink_ref[...]
    for c in range(tq // BLOCK):
        q = q_ref[0, c * BLOCK:(c + 1) * BLOCK, :]
        parts = []
        for j in range(nj):
            qj = q[:, j * LANES:(j + 1) * LANES]
            parts.append(jnp.where(lane_q < SWA_HEAD_DIM, qj, jnp.zeros_like(qj)))
            parts.append(jnp.where(lane_q >= SWA_HEAD_DIM, qj, jnp.zeros_like(qj)))
        q2 = jnp.concatenate(parts, axis=0)
        k = kk[c * BLOCK:(c + 2) * BLOCK]
        v = vv[c * BLOCK:(c + 2) * BLOCK]
        s = lax.dot_general(q2, k, (((1,), (1,)), ((), ())), preferred_element_type=F32)
        allowed = in_window
        if c == 0:
            allowed = allowed & (kj >= jnp.where(i > 0, 0, BLOCK))
        s = jnp.where(allowed, s, MASK_VALUE)
        m = jnp.maximum(jnp.max(s, axis=-1, keepdims=True), sink)
        p = jnp.exp(s - m)
        denom = jnp.sum(p, axis=-1, keepdims=True) + jnp.exp(sink - m)
        o2 = jnp.dot(p.astype(BF16), v, preferred_element_type=F32) / denom
        outs = []
        for j in range(nj):
            lo = o2[(2 * j) * BLOCK:(2 * j + 1) * BLOCK]
            hi = o2[(2 * j + 1) * BLOCK:(2 * j + 2) * BLOCK]
            outs.append(jnp.where(lane_q < SWA_HEAD_DIM, lo, hi))
        o_ref[0, c * BLOCK:(c + 1) * BLOCK, :] = jnp.concatenate(outs, axis=1).astype(BF16)


def _swa_attention(q, kv, sink_rows, *, tq=512):
    b, s, _ = q.shape
    per = tq // BLOCK
    cur = lambda bi, i: (bi, i, 0)
    prev = lambda bi, i: (bi, jnp.maximum(i * per - 1, 0), 0)
    cur_v = lambda bi, i: (bi, i, 1)
    prev_v = lambda bi, i: (bi, jnp.maximum(i * per - 1, 0), 1)
    rows = sink_rows.shape[0]
    return pl.pallas_call(
        functools.partial(_swa_kernel, tq=tq),
        out_shape=jax.ShapeDtypeStruct((b, s, D_MODEL), BF16),
        grid=(b, s // tq),
        in_specs=[
            pl.BlockSpec((rows, 1), lambda bi, i: (0, 0)),
            pl.BlockSpec((1, tq, D_MODEL), cur),
            pl.BlockSpec((1, BLOCK, LANES), prev),
            pl.BlockSpec((1, tq, LANES), cur),
            pl.BlockSpec((1, BLOCK, LANES), prev_v),
            pl.BlockSpec((1, tq, LANES), cur_v),
        ],
        out_specs=pl.BlockSpec((1, tq, D_MODEL), cur),
        compiler_params=_params("arbitrary", "arbitrary"),
        name="swa_attention",
    )(sink_rows, q, kv, kv, kv, kv)


def _final_norm_kernel(h_ref, g_ref, o_ref):
    o_ref[...] = _rms_norm(h_ref[...], g_ref[...])


def _final_norm(h, g, *, tm=1024):
    n = h.shape[0]
    row = pl.BlockSpec((tm, D_MODEL), lambda i: (i, 0))
    return pl.pallas_call(
        _final_norm_kernel,
        out_shape=jax.ShapeDtypeStruct((n, D_MODEL), F32),
        grid=(n // tm,),
        in_specs=[row, pl.BlockSpec((1, D_MODEL), lambda i: (0, 0))],
        out_specs=row,
        compiler_params=_params("arbitrary"),
        name="final_norm",
    )(h, g.reshape(1, D_MODEL))


def _paired_head_order():
    cols = []
    for j in range(SWA_GROUP):
        cols.extend(range(j * SWA_HEAD_DIM, (j + 1) * SWA_HEAD_DIM))
        cols.extend(range((SWA_GROUP + j) * SWA_HEAD_DIM, (SWA_GROUP + j + 1) * SWA_HEAD_DIM))
    return jnp.asarray(cols, dtype=jnp.int32)


def kernel(x, positions, ln_ffn1, ffn1_w_gate, ffn1_w_up, ffn1_w_down, ln_mix, ln_ffn2, ffn2_w_gate, ffn2_w_up, ffn2_w_down, a_w_qkv, a_w_o, a_lambda_q1, a_lambda_k1, a_lambda_q2, a_lambda_k2, a_subln, b_w_q, b_b_q, b_sinks, b_w_o, b_b_o, kv_norm, kv_w_k, kv_b_k, kv_w_v, kv_b_v, final_norm):
    batch, seq, _ = x.shape
    n = batch * seq
    tables = _rope_tables(positions)
    h = x.reshape(n, D_MODEL)
    perm = _paired_head_order()
    kv = None
    for layer in range(DEPTH):
        h = _ffn(h, ln_ffn1[layer], ffn1_w_gate[layer].astype(BF16), ffn1_w_up[layer].astype(BF16),
                 ffn1_w_down[layer].astype(BF16))
        if layer < N_A_LAYERS:
            a = layer
            lambda_init = 0.8 - 0.6 * math.exp(-0.3 * layer)
            qkv = _proj(h, ln_mix[layer], a_w_qkv[a].astype(BF16), jnp.zeros((3 * D_MODEL,), F32),
                        tables, rope_cols=2 * D_MODEL, scale_cols=D_MODEL,
                        scale=DIFF_SUB_DIM ** -0.5 * LOG2_E)
            lam_vecs = jnp.stack([a_lambda_q1[a], a_lambda_k1[a], a_lambda_q2[a], a_lambda_k2[a]])
            o = _diff_attention(qkv.reshape(batch, seq, 3 * D_MODEL), lam_vecs, a_subln[a],
                                lambda_init)
            h = _out_proj(h, o.reshape(n, D_MODEL), a_w_o[a].astype(BF16),
                          jnp.zeros((D_MODEL,), F32))
        else:
            b = layer - N_A_LAYERS
            q = _proj(h, ln_mix[layer], b_w_q[b][:, perm].astype(BF16), b_b_q[b][perm], tables,
                      rope_cols=D_MODEL, scale_cols=D_MODEL, scale=SWA_HEAD_DIM ** -0.5)
            sink_pairs = jnp.stack([b_sinks[b][:SWA_GROUP], b_sinks[b][SWA_GROUP:]], axis=1)
            sink_rows = jnp.repeat(sink_pairs.reshape(-1), BLOCK).reshape(-1, 1)
            o = _swa_attention(q.reshape(batch, seq, D_MODEL), kv, sink_rows)
            h = _out_proj(h, o.reshape(n, D_MODEL), b_w_o[b][perm, :].astype(BF16), b_b_o[b])
        h = _ffn(h, ln_ffn2[layer], ffn2_w_gate[layer].astype(BF16), ffn2_w_up[layer].astype(BF16),
                 ffn2_w_down[layer].astype(BF16))
        if layer == N_A_LAYERS - 1:
            w_kv = jnp.concatenate([kv_w_k, kv_w_v], axis=1).astype(BF16)
            b_kv = jnp.concatenate([kv_b_k, kv_b_v])
            kv = _proj(h, kv_norm, w_kv, b_kv, tables, rope_cols=SWA_KV_WIDTH, scale_cols=0,
                       scale=1.0).reshape(batch, seq, 2 * SWA_KV_WIDTH)
    return _final_norm(h, final_norm).reshape(batch, seq, D_MODEL)
```

```python
import functools
import math

import jax
import jax.numpy as jnp
from jax import lax
from jax.experimental import pallas as pl
from jax.experimental.pallas import tpu as pltpu

D_MODEL = 1024
D_FF = 2816
DEPTH = 4
N_A_LAYERS = 2
DIFF_HEADS = 8
DIFF_SUB_DIM = 64
DIFF_V_DIM = 128
SWA_Q_HEADS = 16
SWA_KV_HEADS = 2
SWA_HEAD_DIM = 64
SWA_GROUP = SWA_Q_HEADS // SWA_KV_HEADS
SWA_KV_WIDTH = SWA_KV_HEADS * SWA_HEAD_DIM
WINDOW = 128
BLOCK = 128
ROPE_THETA = 500000.0
ROT_DIM = 16
ROT_HALF = ROT_DIM // 2
NORM_EPS = 1e-5

LANES = 128
VMEM_LIMIT = 56 * 1024 * 1024
MASK_VALUE = -1e30
LOG2_E = math.log2(math.e)

F32 = jnp.float32
BF16 = jnp.bfloat16


def _params(*sem):
    return pltpu.CompilerParams(dimension_semantics=sem, vmem_limit_bytes=VMEM_LIMIT)


def _resident(shape):
    return pl.BlockSpec(shape, lambda *_: (0,) * len(shape), pipeline_mode=pl.Buffered(1))


def _rms_norm(x, g):
    ms = jnp.mean(x * x, axis=-1, keepdims=True)
    return x * lax.rsqrt(ms + NORM_EPS) * g


def _rope_table_kernel(pos_ref, invf_ref, c_ref, s1_ref, s2_ref):
    ang = pos_ref[...] * invf_ref[...]
    cos = jnp.cos(ang)
    sin = jnp.sin(ang)
    lane = lax.broadcasted_iota(jnp.int32, ang.shape, 1) % SWA_HEAD_DIM
    first = lane < ROT_HALF
    second = (lane >= ROT_HALF) & (lane < ROT_DIM)
    c_ref[...] = jnp.where(first | second, cos, 1.0)
    s1_ref[...] = jnp.where(first, -sin, 0.0)
    s2_ref[...] = jnp.where(second, sin, 0.0)


def _rope_tables(positions, tm=1024):
    n = positions.size
    pos = jnp.broadcast_to(positions.reshape(n, 1).astype(F32), (n, LANES))
    inv_freq = ROPE_THETA ** (-jnp.arange(0, ROT_DIM, 2, dtype=F32) / ROT_DIM)
    invf = jnp.tile(inv_freq, LANES // ROT_HALF).reshape(1, LANES)
    spec = pl.BlockSpec((tm, LANES), lambda i: (i, 0))
    out = jax.ShapeDtypeStruct((n, LANES), F32)
    return pl.pallas_call(
        _rope_table_kernel,
        out_shape=(out, out, out),
        grid=(n // tm,),
        in_specs=[spec, pl.BlockSpec((1, LANES), lambda i: (0, 0))],
        out_specs=(spec, spec, spec),
        compiler_params=_params("arbitrary"),
        name="rope_tables",
    )(pos, invf)


def _apply_rope(x, c, s1, s2):
    return (x * c + pltpu.roll(x, LANES - ROT_HALF, 1) * s1 + pltpu.roll(x, ROT_HALF, 1) * s2)


def _ffn_kernel(h_ref, g_ref, wg_ref, wu_ref, wd_ref, o_ref, xn_ref, acc_ref, *, tf):
    xn_ref[...] = _rms_norm(h_ref[...], g_ref[...]).astype(BF16)
    for c in range(D_FF // tf):
        cols = slice(c * tf, (c + 1) * tf)
        xn = xn_ref[...]
        gate = jnp.dot(xn, wg_ref[:, cols], preferred_element_type=F32)
        up = jnp.dot(xn, wu_ref[:, cols], preferred_element_type=F32)
        act = (gate / (1.0 + jnp.exp(-gate)) * up).astype(BF16)
        down = jnp.dot(act, wd_ref[cols, :], preferred_element_type=F32)
        if c == 0:
            acc_ref[...] = down
        else:
            acc_ref[...] += down
    o_ref[...] = h_ref[...] + 0.5 * acc_ref[...]


def _ffn(h, g, wg, wu, wd, *, tm=512, tf=256):
    n = h.shape[0]
    row = pl.BlockSpec((tm, D_MODEL), lambda i: (i, 0))
    return pl.pallas_call(
        functools.partial(_ffn_kernel, tf=tf),
        out_shape=jax.ShapeDtypeStruct((n, D_MODEL), F32),
        grid=(n // tm,),
        in_specs=[row, _resident((1, D_MODEL)), _resident((D_MODEL, D_FF)),
                  _resident((D_MODEL, D_FF)), _resident((D_FF, D_MODEL))],
        out_specs=row,
        scratch_shapes=[pltpu.VMEM((tm, D_MODEL), BF16), pltpu.VMEM((tm, D_MODEL), F32)],
        compiler_params=_params("arbitrary"),
        name="ffn",
    )(h, g.reshape(1, D_MODEL), wg, wu, wd)


def _proj_kernel(h_ref, g_ref, w_ref, b_ref, c_ref, s1_ref, s2_ref, o_ref, xn_ref, *,
                 n_out, rope_cols, scale_cols, scale, chunk, head_major):
    xn_ref[...] = _rms_norm(h_ref[...], g_ref[...]).astype(BF16)
    c, s1, s2 = c_ref[...], s1_ref[...], s2_ref[...]
    for c0 in range(0, n_out, chunk):
        cols = slice(c0, c0 + chunk)
        y = jnp.dot(xn_ref[...], w_ref[:, cols], preferred_element_type=F32) + b_ref[:, cols]
        for j in range(0, chunk, LANES):
            yj = y[:, j:j + LANES]
            if c0 + j < rope_cols:
                yj = _apply_rope(yj, c, s1, s2)
            if c0 + j < scale_cols:
                yj = yj * scale
            if head_major:
                o_ref[(c0 + j) // LANES] = yj.astype(BF16)
            else:
                o_ref[:, c0 + j:c0 + j + LANES] = yj.astype(BF16)


def _proj(h, g, w, b, tables, *, rope_cols, scale_cols, scale, head_major=False, tm=512):
    n = h.shape[0]
    n_out = w.shape[1]
    chunk = min(n_out, 512)
    assert rope_cols % LANES == 0 and scale_cols % LANES == 0 and n_out % chunk == 0
    row = pl.BlockSpec((tm, D_MODEL), lambda i: (i, 0))
    tab = pl.BlockSpec((tm, LANES), lambda i: (i, 0))
    if head_major:
        out_shape = jax.ShapeDtypeStruct((n_out // LANES, n, LANES), BF16)
        out_spec = pl.BlockSpec((n_out // LANES, tm, LANES), lambda i: (0, i, 0))
    else:
        out_shape = jax.ShapeDtypeStruct((n, n_out), BF16)
        out_spec = pl.BlockSpec((tm, n_out), lambda i: (i, 0))
    return pl.pallas_call(
        functools.partial(_proj_kernel, n_out=n_out, rope_cols=rope_cols, scale_cols=scale_cols,
                          scale=scale, chunk=chunk, head_major=head_major),
        out_shape=out_shape,
        grid=(n // tm,),
        in_specs=[row, _resident((1, D_MODEL)), _resident((D_MODEL, n_out)), _resident((1, n_out)),
                  tab, tab, tab],
        out_specs=out_spec,
        scratch_shapes=[pltpu.VMEM((tm, D_MODEL), BF16)],
        compiler_params=_params("arbitrary"),
        name="norm_proj",
    )(h, g.reshape(1, D_MODEL), w, b.reshape(1, n_out), *tables)


def _out_proj_kernel(h_ref, o_ref, w_ref, b_ref, out_ref):
    out_ref[...] = (h_ref[...] + b_ref[...]
                    + jnp.dot(o_ref[...], w_ref[...], preferred_element_type=F32))


def _out_proj(h, o, w, b, *, tm=512):
    n = h.shape[0]
    row = pl.BlockSpec((tm, D_MODEL), lambda i: (i, 0))
    return pl.pallas_call(
        _out_proj_kernel,
        out_shape=jax.ShapeDtypeStruct((n, D_MODEL), F32),
        grid=(n // tm,),
        in_specs=[row, row, _resident((D_MODEL, D_MODEL)), _resident((1, D_MODEL))],
        out_specs=row,
        compiler_params=_params("arbitrary"),
        name="out_proj",
    )(h, o, w, b.reshape(1, D_MODEL))


def _diff_attn_kernel(lam_ref, gsub_ref, q_ref, k_ref, v_ref, o_ref, s_ref, p_ref, m_ref, acc_ref, *,
                      t, lambda_init):
    i = pl.program_id(2)
    q = q_ref[0]
    lane = lax.broadcasted_iota(jnp.int32, q.shape, 1)
    zero = jnp.zeros_like(q)
    q2 = jnp.concatenate([jnp.where(lane < DIFF_SUB_DIM, q, zero),
                          jnp.where(lane >= DIFF_SUB_DIM, q, zero)], axis=0)
    ones = jnp.ones((t, LANES), BF16)

    def scores(j):
        k = k_ref[0, pl.ds(j * t, t), :]
        return lax.dot_general(q2, k, (((1,), (1,)), ((), ())), preferred_element_type=F32)

    def weighted_values(j):
        v_aug = jnp.concatenate([v_ref[0, pl.ds(j * t, t), :], ones], axis=1)
        return jnp.dot(p_ref[...], v_aug, preferred_element_type=F32)

    def softmax_tile(masked):
        s = s_ref[...]
        if masked:
            row = lax.broadcasted_iota(jnp.int32, s.shape, 0) % t
            col = lax.broadcasted_iota(jnp.int32, s.shape, 1)
            s = jnp.where(col <= row, s, MASK_VALUE)
        m_prev = m_ref[...]
        m_next = jnp.maximum(m_prev, jnp.max(s, axis=-1, keepdims=True))
        m_ref[...] = m_next
        alpha = jnp.exp2(m_prev - m_next)
        p = jnp.exp2(s - jnp.tile(m_next, (1, t // LANES)))
        return jnp.tile(alpha, (1, 2)), p.astype(BF16)

    m_ref[...] = jnp.full(m_ref.shape, MASK_VALUE, F32)
    acc_ref[...] = jnp.zeros(acc_ref.shape, F32)
    p_ref[...] = jnp.zeros(p_ref.shape, BF16)
    s_ref[...] = scores(0)

    def body(j, carry):
        pv_prev = weighted_values(jnp.maximum(j - 1, 0))
        alpha, p = softmax_tile(False)
        s_ref[...] = scores(j + 1)
        p_ref[...] = p
        acc_ref[...] = alpha * (acc_ref[...] + pv_prev)
        return carry

    lax.fori_loop(0, i, body, 0)
    pv_prev = weighted_values(jnp.maximum(i - 1, 0))
    alpha, p = softmax_tile(True)
    p_ref[...] = p
    acc = alpha * (acc_ref[...] + pv_prev) + weighted_values(i)

    lv = lam_ref[...]
    lam = (jnp.exp(jnp.sum(lv[0:1] * lv[1:2], axis=-1, keepdims=True))
           - jnp.exp(jnp.sum(lv[2:3] * lv[3:4], axis=-1, keepdims=True)) + lambda_init)
    attn = acc[:, :DIFF_V_DIM] / acc[:, DIFF_V_DIM:]
    o = attn[:t] - lam * attn[t:]
    o = _rms_norm(o, gsub_ref[...]) * (1.0 - lambda_init)
    o_ref[0] = o.astype(BF16)


def _diff_attention(qkv, lam_vecs, g_sub, lambda_init, *, t=256):
    _, b, s, _ = qkv.shape
    h = DIFF_HEADS
    return pl.pallas_call(
        functools.partial(_diff_attn_kernel, t=t, lambda_init=lambda_init),
        out_shape=jax.ShapeDtypeStruct((b, s, D_MODEL), BF16),
        grid=(b, h, s // t),
        in_specs=[
            pl.BlockSpec((4, DIFF_SUB_DIM), lambda bi, hi, i: (0, 0)),
            pl.BlockSpec((1, DIFF_V_DIM), lambda bi, hi, i: (0, 0)),
            pl.BlockSpec((None, 1, t, DIFF_V_DIM), lambda bi, hi, i: (hi, bi, i, 0)),
            pl.BlockSpec((None, 1, s, DIFF_V_DIM), lambda bi, hi, i: (h + hi, bi, 0, 0)),
            pl.BlockSpec((None, 1, s, DIFF_V_DIM), lambda bi, hi, i: (2 * h + hi, bi, 0, 0)),
        ],
        out_specs=pl.BlockSpec((1, t, DIFF_V_DIM), lambda bi, hi, i: (bi, i, hi)),
        scratch_shapes=[pltpu.VMEM((2 * t, t), F32), pltpu.VMEM((2 * t, t), BF16),
                        pltpu.VMEM((2 * t, LANES), F32), pltpu.VMEM((2 * t, 2 * DIFF_V_DIM), F32)],
        compiler_params=_params("arbitrary", "arbitrary", "arbitrary"),
        name="diff_attention",
    )(lam_vecs, g_sub.reshape(1, DIFF_V_DIM), qkv, qkv, qkv)


def _swa_kernel(sink_ref, q_ref, kp_ref, kc_ref, vp_ref, vc_ref, o_ref, *, tq):
    i = pl.program_id(1)
    kk = jnp.concatenate([kp_ref[0], kc_ref[0]], axis=0)
    vv = jnp.concatenate([vp_ref[0], vc_ref[0]], axis=0)
    nj = D_MODEL // LANES
    rows = 2 * nj * BLOCK
    lane_q = lax.broadcasted_iota(jnp.int32, (BLOCK, LANES), 1)
    qi = lax.broadcasted_iota(jnp.int32, (rows, 2 * BLOCK), 0) % BLOCK
    kj = lax.broadcasted_iota(jnp.int32, (rows, 2 * BLOCK), 1)
    in_window = (kj > qi) & (kj <= qi + WINDOW)
    sink = sink_ref[...]
    for c in range(tq // BLOCK):
        q = q_ref[0, c * BLOCK:(c + 1) * BLOCK, :]
        parts = []
        for j in range(nj):
            qj = q[:, j * LANES:(j + 1) * LANES]
            parts.append(jnp.where(lane_q < SWA_HEAD_DIM, qj, jnp.zeros_like(qj)))
            parts.append(jnp.where(lane_q >= SWA_HEAD_DIM, qj, jnp.zeros_like(qj)))
        q2 = jnp.concatenate(parts, axis=0)
        k = kk[c * BLOCK:(c + 2) * BLOCK]
        v = vv[c * BLOCK:(c + 2) * BLOCK]
        s = lax.dot_general(q2, k, (((1,), (1,)), ((), ())), preferred_element_type=F32)
        allowed = in_window
        if c == 0:
            allowed = allowed & (kj >= jnp.where(i > 0, 0, BLOCK))
        s = jnp.where(allowed, s, MASK_VALUE)
        m = jnp.maximum(jnp.max(s, axis=-1, keepdims=True), sink)
        p = jnp.exp(s - m)
        denom = jnp.sum(p, axis=-1, keepdims=True) + jnp.exp(sink - m)
        o2 = jnp.dot(p.astype(BF16), v, preferred_element_type=F32) / denom
        outs = []
        for j in range(nj):
            lo = o2[(2 * j) * BLOCK:(2 * j + 1) * BLOCK]
            hi = o2[(2 * j + 1) * BLOCK:(2 * j + 2) * BLOCK]
            outs.append(jnp.where(lane_q < SWA_HEAD_DIM, lo, hi))
        o_ref[0, c * BLOCK:(c + 1) * BLOCK, :] = jnp.concatenate(outs, axis=1).astype(BF16)


def _swa_attention(q, kv, sink_rows, *, tq=512):
    b, s, _ = q.shape
    per = tq // BLOCK
    cur = lambda bi, i: (bi, i, 0)
    prev = lambda bi, i: (bi, jnp.maximum(i * per - 1, 0), 0)
    cur_v = lambda bi, i: (bi, i, 1)
    prev_v = lambda bi, i: (bi, jnp.maximum(i * per - 1, 0), 1)
    rows = sink_rows.shape[0]
    return pl.pallas_call(
        functools.partial(_swa_kernel, tq=tq),
        out_shape=jax.ShapeDtypeStruct((b, s, D_MODEL), BF16),
        grid=(b, s // tq),
        in_specs=[
            pl.BlockSpec((rows, 1), lambda bi, i: (0, 0)),
            pl.BlockSpec((1, tq, D_MODEL), cur),
            pl.BlockSpec((1, BLOCK, LANES), prev),
            pl.BlockSpec((1, tq, LANES), cur),
            pl.BlockSpec((1, BLOCK, LANES), prev_v),
            pl.BlockSpec((1, tq, LANES), cur_v),
        ],
        out_specs=pl.BlockSpec((1, tq, D_MODEL), cur),
        compiler_params=_params("arbitrary", "arbitrary"),
        name="swa_attention",
    )(sink_rows, q, kv, kv, kv, kv)


def _final_norm_kernel(h_ref, g_ref, o_ref):
    o_ref[...] = _rms_norm(h_ref[...], g_ref[...])


def _final_norm(h, g, *, tm=1024):
    n = h.shape[0]
    row = pl.BlockSpec((tm, D_MODEL), lambda i: (i, 0))
    return pl.pallas_call(
        _final_norm_kernel,
        out_shape=jax.ShapeDtypeStruct((n, D_MODEL), F32),
        grid=(n // tm,),
        in_specs=[row, pl.BlockSpec((1, D_MODEL), lambda i: (0, 0))],
        out_specs=row,
        compiler_params=_params("arbitrary"),
        name="final_norm",
    )(h, g.reshape(1, D_MODEL))


def _paired_head_order():
    cols = []
    for j in range(SWA_GROUP):
        cols.extend(range(j * SWA_HEAD_DIM, (j + 1) * SWA_HEAD_DIM))
        cols.extend(range((SWA_GROUP + j) * SWA_HEAD_DIM, (SWA_GROUP + j + 1) * SWA_HEAD_DIM))
    return jnp.asarray(cols, dtype=jnp.int32)


def kernel(x, positions, ln_ffn1, ffn1_w_gate, ffn1_w_up, ffn1_w_down, ln_mix, ln_ffn2, ffn2_w_gate, ffn2_w_up, ffn2_w_down, a_w_qkv, a_w_o, a_lambda_q1, a_lambda_k1, a_lambda_q2, a_lambda_k2, a_subln, b_w_q, b_b_q, b_sinks, b_w_o, b_b_o, kv_norm, kv_w_k, kv_b_k, kv_w_v, kv_b_v, final_norm):
    batch, seq, _ = x.shape
    n = batch * seq
    tables = _rope_tables(positions)
    h = x.reshape(n, D_MODEL)
    perm = _paired_head_order()
    kv = None
    for layer in range(DEPTH):
        h = _ffn(h, ln_ffn1[layer], ffn1_w_gate[layer].astype(BF16), ffn1_w_up[layer].astype(BF16),
                 ffn1_w_down[layer].astype(BF16))
        if layer < N_A_LAYERS:
            a = layer
            lambda_init = 0.8 - 0.6 * math.exp(-0.3 * layer)
            qkv = _proj(h, ln_mix[layer], a_w_qkv[a].astype(BF16), jnp.zeros((3 * D_MODEL,), F32),
                        tables, rope_cols=2 * D_MODEL, scale_cols=D_MODEL,
                        scale=DIFF_SUB_DIM ** -0.5 * LOG2_E, head_major=True)
            lam_vecs = jnp.stack([a_lambda_q1[a], a_lambda_k1[a], a_lambda_q2[a], a_lambda_k2[a]])
            o = _diff_attention(qkv.reshape(3 * DIFF_HEADS, batch, seq, DIFF_V_DIM), lam_vecs,
                                a_subln[a], lambda_init)
            h = _out_proj(h, o.reshape(n, D_MODEL), a_w_o[a].astype(BF16),
                          jnp.zeros((D_MODEL,), F32))
        else:
            b = layer - N_A_LAYERS
            q = _proj(h, ln_mix[layer], b_w_q[b][:, perm].astype(BF16), b_b_q[b][perm], tables,
                      rope_cols=D_MODEL, scale_cols=D_MODEL, scale=SWA_HEAD_DIM ** -0.5)
            sink_pairs = jnp.stack([b_sinks[b][:SWA_GROUP], b_sinks[b][SWA_GROUP:]], axis=1)
            sink_rows = jnp.repeat(sink_pairs.reshape(-1), BLOCK).reshape(-1, 1)
            o = _swa_attention(q.reshape(batch, seq, D_MODEL), kv, sink_rows)
            h = _out_proj(h, o.reshape(n, D_MODEL), b_w_o[b][perm, :].astype(BF16), b_b_o[b])
        h = _ffn(h, ln_ffn2[layer], ffn2_w_gate[layer].astype(BF16), ffn2_w_up[layer].astype(BF16),
                 ffn2_w_down[layer].astype(BF16))
        if layer == N_A_LAYERS - 1:
            w_kv = jnp.concatenate([kv_w_k, kv_w_v], axis=1).astype(BF16)
            b_kv = jnp.concatenate([kv_b_k, kv_b_v])
            kv = _proj(h, kv_norm, w_kv, b_kv, tables, rope_cols=SWA_KV_WIDTH, scale_cols=0,
                       scale=1.0).reshape(batch, seq, 2 * SWA_KV_WIDTH)
    return _final_norm(h, final_norm).reshape(batch, seq, D_MODEL)
```

```python
import functools
import math

import jax
import jax.numpy as jnp
from jax import lax
from jax.experimental import pallas as pl
from jax.experimental.pallas import tpu as pltpu

D_MODEL = 1024
D_FF = 2816
DEPTH = 4
N_A_LAYERS = 2
DIFF_HEADS = 8
DIFF_SUB_DIM = 64
DIFF_V_DIM = 128
SWA_Q_HEADS = 16
SWA_KV_HEADS = 2
SWA_HEAD_DIM = 64
SWA_GROUP = SWA_Q_HEADS // SWA_KV_HEADS
SWA_KV_WIDTH = SWA_KV_HEADS * SWA_HEAD_DIM
WINDOW = 128
BLOCK = 128
ROPE_THETA = 500000.0
ROT_DIM = 16
ROT_HALF = ROT_DIM // 2
NORM_EPS = 1e-5

LANES = 128
VMEM_LIMIT = 56 * 1024 * 1024
MASK_VALUE = -1e30
LOG2_E = math.log2(math.e)

F32 = jnp.float32
BF16 = jnp.bfloat16


def _params(*sem):
    return pltpu.CompilerParams(dimension_semantics=sem, vmem_limit_bytes=VMEM_LIMIT)


def _resident(shape):
    return pl.BlockSpec(shape, lambda *_: (0,) * len(shape), pipeline_mode=pl.Buffered(1))


def _rms_norm(x, g):
    ms = jnp.mean(x * x, axis=-1, keepdims=True)
    return x * lax.rsqrt(ms + NORM_EPS) * g


def _rope_table_kernel(pos_ref, invf_ref, c_ref, s1_ref, s2_ref):
    ang = pos_ref[...] * invf_ref[...]
    cos = jnp.cos(ang)
    sin = jnp.sin(ang)
    lane = lax.broadcasted_iota(jnp.int32, ang.shape, 1) % SWA_HEAD_DIM
    first = lane < ROT_HALF
    second = (lane >= ROT_HALF) & (lane < ROT_DIM)
    c_ref[...] = jnp.where(first | second, cos, 1.0)
    s1_ref[...] = jnp.where(first, -sin, 0.0)
    s2_ref[...] = jnp.where(second, sin, 0.0)


def _rope_tables(positions, tm=1024):
    n = positions.size
    pos = jnp.broadcast_to(positions.reshape(n, 1).astype(F32), (n, LANES))
    inv_freq = ROPE_THETA ** (-jnp.arange(0, ROT_DIM, 2, dtype=F32) / ROT_DIM)
    invf = jnp.tile(inv_freq, LANES // ROT_HALF).reshape(1, LANES)
    spec = pl.BlockSpec((tm, LANES), lambda i: (i, 0))
    out = jax.ShapeDtypeStruct((n, LANES), F32)
    return pl.pallas_call(
        _rope_table_kernel,
        out_shape=(out, out, out),
        grid=(n // tm,),
        in_specs=[spec, pl.BlockSpec((1, LANES), lambda i: (0, 0))],
        out_specs=(spec, spec, spec),
        compiler_params=_params("arbitrary"),
        name="rope_tables",
    )(pos, invf)


def _apply_rope(x, c, s1, s2):
    return (x * c + pltpu.roll(x, LANES - ROT_HALF, 1) * s1 + pltpu.roll(x, ROT_HALF, 1) * s2)


def _ffn_kernel(h_ref, g_ref, wg_ref, wu_ref, wd_ref, o_ref, xn_ref, acc_ref, *, tf):
    xn_ref[...] = _rms_norm(h_ref[...], g_ref[...]).astype(BF16)
    for c in range(D_FF // tf):
        cols = slice(c * tf, (c + 1) * tf)
        xn = xn_ref[...]
        gate = jnp.dot(xn, wg_ref[:, cols], preferred_element_type=F32)
        up = jnp.dot(xn, wu_ref[:, cols], preferred_element_type=F32)
        act = (gate / (1.0 + jnp.exp(-gate)) * up).astype(BF16)
        down = jnp.dot(act, wd_ref[cols, :], preferred_element_type=F32)
        if c == 0:
            acc_ref[...] = down
        else:
            acc_ref[...] += down
    o_ref[...] = h_ref[...] + 0.5 * acc_ref[...]


def _ffn(h, g, wg, wu, wd, *, tm=512, tf=256):
    n = h.shape[0]
    row = pl.BlockSpec((tm, D_MODEL), lambda i: (i, 0))
    return pl.pallas_call(
        functools.partial(_ffn_kernel, tf=tf),
        out_shape=jax.ShapeDtypeStruct((n, D_MODEL), F32),
        grid=(n // tm,),
        in_specs=[row, _resident((1, D_MODEL)), _resident((D_MODEL, D_FF)),
                  _resident((D_MODEL, D_FF)), _resident((D_FF, D_MODEL))],
        out_specs=row,
        scratch_shapes=[pltpu.VMEM((tm, D_MODEL), BF16), pltpu.VMEM((tm, D_MODEL), F32)],
        compiler_params=_params("arbitrary"),
        name="ffn",
    )(h, g.reshape(1, D_MODEL), wg, wu, wd)


def _proj_kernel(h_ref, g_ref, w_ref, b_ref, c_ref, s1_ref, s2_ref, o_ref, xn_ref, *,
                 n_out, rope_cols, scale_cols, scale, chunk, head_major):
    xn_ref[...] = _rms_norm(h_ref[...], g_ref[...]).astype(BF16)
    c, s1, s2 = c_ref[...], s1_ref[...], s2_ref[...]
    for c0 in range(0, n_out, chunk):
        cols = slice(c0, c0 + chunk)
        y = jnp.dot(xn_ref[...], w_ref[:, cols], preferred_element_type=F32) + b_ref[:, cols]
        for j in range(0, chunk, LANES):
            yj = y[:, j:j + LANES]
            if c0 + j < rope_cols:
                yj = _apply_rope(yj, c, s1, s2)
            if c0 + j < scale_cols:
                yj = yj * scale
            if head_major:
                o_ref[(c0 + j) // LANES] = yj.astype(BF16)
            else:
                o_ref[:, c0 + j:c0 + j + LANES] = yj.astype(BF16)


def _proj(h, g, w, b, tables, *, rope_cols, scale_cols, scale, head_major=False, tm=512):
    n = h.shape[0]
    n_out = w.shape[1]
    chunk = min(n_out, 512)
    assert rope_cols % LANES == 0 and scale_cols % LANES == 0 and n_out % chunk == 0
    row = pl.BlockSpec((tm, D_MODEL), lambda i: (i, 0))
    tab = pl.BlockSpec((tm, LANES), lambda i: (i, 0))
    if head_major:
        out_shape = jax.ShapeDtypeStruct((n_out // LANES, n, LANES), BF16)
        out_spec = pl.BlockSpec((n_out // LANES, tm, LANES), lambda i: (0, i, 0))
    else:
        out_shape = jax.ShapeDtypeStruct((n, n_out), BF16)
        out_spec = pl.BlockSpec((tm, n_out), lambda i: (i, 0))
    return pl.pallas_call(
        functools.partial(_proj_kernel, n_out=n_out, rope_cols=rope_cols, scale_cols=scale_cols,
                          scale=scale, chunk=chunk, head_major=head_major),
        out_shape=out_shape,
        grid=(n // tm,),
        in_specs=[row, _resident((1, D_MODEL)), _resident((D_MODEL, n_out)), _resident((1, n_out)),
                  tab, tab, tab],
        out_specs=out_spec,
        scratch_shapes=[pltpu.VMEM((tm, D_MODEL), BF16)],
        compiler_params=_params("arbitrary"),
        name="norm_proj",
    )(h, g.reshape(1, D_MODEL), w, b.reshape(1, n_out), *tables)


def _out_proj_kernel(h_ref, o_ref, w_ref, b_ref, out_ref):
    out_ref[...] = (h_ref[...] + b_ref[...]
                    + jnp.dot(o_ref[...], w_ref[...], preferred_element_type=F32))


def _out_proj(h, o, w, b, *, tm=512):
    n = h.shape[0]
    row = pl.BlockSpec((tm, D_MODEL), lambda i: (i, 0))
    return pl.pallas_call(
        _out_proj_kernel,
        out_shape=jax.ShapeDtypeStruct((n, D_MODEL), F32),
        grid=(n // tm,),
        in_specs=[row, row, _resident((D_MODEL, D_MODEL)), _resident((1, D_MODEL))],
        out_specs=row,
        compiler_params=_params("arbitrary"),
        name="out_proj",
    )(h, o, w, b.reshape(1, D_MODEL))


def _diff_attn_kernel(lam_ref, gsub_ref, q_ref, k_ref, v_ref, o_ref, s_ref, p_ref, m_ref, acc_ref, *,
                      t, hp, lambda_init):
    i = pl.program_id(2)
    lane = lax.broadcasted_iota(jnp.int32, (t, LANES), 1)
    ones = jnp.ones((t, LANES), BF16)
    q2 = []
    for hh in range(hp):
        q = q_ref[hh, 0]
        zero = jnp.zeros_like(q)
        q2.append(jnp.concatenate([jnp.where(lane < DIFF_SUB_DIM, q, zero),
                                   jnp.where(lane >= DIFF_SUB_DIM, q, zero)], axis=0))

    def scores(hh, j):
        k = k_ref[hh, 0, pl.ds(j * t, t), :]
        return lax.dot_general(q2[hh], k, (((1,), (1,)), ((), ())), preferred_element_type=F32)

    def weighted_values(hh, j):
        v_aug = jnp.concatenate([v_ref[hh, 0, pl.ds(j * t, t), :], ones], axis=1)
        return jnp.dot(p_ref[hh].astype(BF16), v_aug, preferred_element_type=F32)

    def softmax_tile(hh, masked):
        s = s_ref[hh]
        if masked:
            row = lax.broadcasted_iota(jnp.int32, s.shape, 0) % t
            col = lax.broadcasted_iota(jnp.int32, s.shape, 1)
            s = jnp.where(col <= row, s, MASK_VALUE)
        m_prev = m_ref[hh]
        m_next = jnp.maximum(m_prev, jnp.max(s, axis=-1, keepdims=True))
        m_ref[hh] = m_next
        alpha = jnp.exp2(m_prev - m_next)
        p = jnp.exp2(s - jnp.tile(m_next, (1, t // LANES)))
        return jnp.tile(alpha, (1, 2)), p

    m_ref[...] = jnp.full(m_ref.shape, MASK_VALUE, F32)
    acc_ref[...] = jnp.zeros(acc_ref.shape, F32)
    p_ref[...] = jnp.zeros(p_ref.shape, F32)
    for hh in range(hp):
        s_ref[hh] = scores(hh, 0)

    def body(j, carry):
        for hh in range(hp):
            pv_prev = weighted_values(hh, jnp.maximum(j - 1, 0))
            alpha, p = softmax_tile(hh, False)
            s_ref[hh] = scores(hh, j + 1)
            p_ref[hh] = p
            acc_ref[hh] = alpha * (acc_ref[hh] + pv_prev)
        return carry

    lax.fori_loop(0, i, body, 0)

    lv = lam_ref[...]
    lam = (jnp.exp(jnp.sum(lv[0:1] * lv[1:2], axis=-1, keepdims=True))
           - jnp.exp(jnp.sum(lv[2:3] * lv[3:4], axis=-1, keepdims=True)) + lambda_init)
    for hh in range(hp):
        pv_prev = weighted_values(hh, jnp.maximum(i - 1, 0))
        alpha, p = softmax_tile(hh, True)
        p_ref[hh] = p
        acc = alpha * (acc_ref[hh] + pv_prev) + weighted_values(hh, i)
        attn = acc[:, :DIFF_V_DIM] / acc[:, DIFF_V_DIM:]
        o = attn[:t] - lam * attn[t:]
        o = _rms_norm(o, gsub_ref[...]) * (1.0 - lambda_init)
        o_ref[0, :, hh * DIFF_V_DIM:(hh + 1) * DIFF_V_DIM] = o.astype(BF16)


def _diff_attention(qkv, lam_vecs, g_sub, lambda_init, *, t=512, hp=2):
    _, b, s, _ = qkv.shape
    hg = DIFF_HEADS // hp
    return pl.pallas_call(
        functools.partial(_diff_attn_kernel, t=t, hp=hp, lambda_init=lambda_init),
        out_shape=jax.ShapeDtypeStruct((b, s, D_MODEL), BF16),
        grid=(b, hg, s // t),
        in_specs=[
            pl.BlockSpec((4, DIFF_SUB_DIM), lambda bi, hi, i: (0, 0)),
            pl.BlockSpec((1, DIFF_V_DIM), lambda bi, hi, i: (0, 0)),
            pl.BlockSpec((hp, 1, t, DIFF_V_DIM), lambda bi, hi, i: (hi, bi, i, 0)),
            pl.BlockSpec((hp, 1, s, DIFF_V_DIM), lambda bi, hi, i: (hg + hi, bi, 0, 0)),
            pl.BlockSpec((hp, 1, s, DIFF_V_DIM), lambda bi, hi, i: (2 * hg + hi, bi, 0, 0)),
        ],
        out_specs=pl.BlockSpec((1, t, hp * DIFF_V_DIM), lambda bi, hi, i: (bi, i, hi)),
        scratch_shapes=[pltpu.VMEM((hp, 2 * t, t), F32), pltpu.VMEM((hp, 2 * t, t), F32),
                        pltpu.VMEM((hp, 2 * t, LANES), F32),
                        pltpu.VMEM((hp, 2 * t, 2 * DIFF_V_DIM), F32)],
        compiler_params=_params("arbitrary", "arbitrary", "arbitrary"),
        name="diff_attention",
    )(lam_vecs, g_sub.reshape(1, DIFF_V_DIM), qkv, qkv, qkv)


def _swa_kernel(sink_ref, q_ref, kp_ref, kc_ref, vp_ref, vc_ref, o_ref, *, tq):
    i = pl.program_id(1)
    kk = jnp.concatenate([kp_ref[0], kc_ref[0]], axis=0)
    vv = jnp.concatenate([vp_ref[0], vc_ref[0]], axis=0)
    nj = D_MODEL // LANES
    rows = 2 * nj * BLOCK
    lane_q = lax.broadcasted_iota(jnp.int32, (BLOCK, LANES), 1)
    qi = lax.broadcasted_iota(jnp.int32, (rows, 2 * BLOCK), 0) % BLOCK
    kj = lax.broadcasted_iota(jnp.int32, (rows, 2 * BLOCK), 1)
    in_window = (kj > qi) & (kj <= qi + WINDOW)
    sink = sink_ref[...]
    for c in range(tq // BLOCK):
        q = q_ref[0, c * BLOCK:(c + 1) * BLOCK, :]
        parts = []
        for j in range(nj):
            qj = q[:, j * LANES:(j + 1) * LANES]
            parts.append(jnp.where(lane_q < SWA_HEAD_DIM, qj, jnp.zeros_like(qj)))
            parts.append(jnp.where(lane_q >= SWA_HEAD_DIM, qj, jnp.zeros_like(qj)))
        q2 = jnp.concatenate(parts, axis=0)
        k = kk[c * BLOCK:(c + 2) * BLOCK]
        v = vv[c * BLOCK:(c + 2) * BLOCK]
        s = lax.dot_general(q2, k, (((1,), (1,)), ((), ())), preferred_element_type=F32)
        allowed = in_window
        if c == 0:
            allowed = allowed & (kj >= jnp.where(i > 0, 0, BLOCK))
        s = jnp.where(allowed, s, MASK_VALUE)
        m = jnp.maximum(jnp.max(s, axis=-1, keepdims=True), sink)
        p = jnp.exp(s - m)
        denom = jnp.sum(p, axis=-1, keepdims=True) + jnp.exp(sink - m)
        o2 = jnp.dot(p.astype(BF16), v, preferred_element_type=F32) / denom
        outs = []
        for j in range(nj):
            lo = o2[(2 * j) * BLOCK:(2 * j + 1) * BLOCK]
            hi = o2[(2 * j + 1) * BLOCK:(2 * j + 2) * BLOCK]
            outs.append(jnp.where(lane_q < SWA_HEAD_DIM, lo, hi))
        o_ref[0, c * BLOCK:(c + 1) * BLOCK, :] = jnp.concatenate(outs, axis=1).astype(BF16)


def _swa_attention(q, kv, sink_rows, *, tq=512):
    b, s, _ = q.shape
    per = tq // BLOCK
    cur = lambda bi, i: (bi, i, 0)
    prev = lambda bi, i: (bi, jnp.maximum(i * per - 1, 0), 0)
    cur_v = lambda bi, i: (bi, i, 1)
    prev_v = lambda bi, i: (bi, jnp.maximum(i * per - 1, 0), 1)
    rows = sink_rows.shape[0]
    return pl.pallas_call(
        functools.partial(_swa_kernel, tq=tq),
        out_shape=jax.ShapeDtypeStruct((b, s, D_MODEL), BF16),
        grid=(b, s // tq),
        in_specs=[
            pl.BlockSpec((rows, 1), lambda bi, i: (0, 0)),
            pl.BlockSpec((1, tq, D_MODEL), cur),
            pl.BlockSpec((1, BLOCK, LANES), prev),
            pl.BlockSpec((1, tq, LANES), cur),
            pl.BlockSpec((1, BLOCK, LANES), prev_v),
            pl.BlockSpec((1, tq, LANES), cur_v),
        ],
        out_specs=pl.BlockSpec((1, tq, D_MODEL), cur),
        compiler_params=_params("arbitrary", "arbitrary"),
        name="swa_attention",
    )(sink_rows, q, kv, kv, kv, kv)


def _final_norm_kernel(h_ref, g_ref, o_ref):
    o_ref[...] = _rms_norm(h_ref[...], g_ref[...])


def _final_norm(h, g, *, tm=1024):
    n = h.shape[0]
    row = pl.BlockSpec((tm, D_MODEL), lambda i: (i, 0))
    return pl.pallas_call(
        _final_norm_kernel,
        out_shape=jax.ShapeDtypeStruct((n, D_MODEL), F32),
        grid=(n // tm,),
        in_specs=[row, pl.BlockSpec((1, D_MODEL), lambda i: (0, 0))],
        out_specs=row,
        compiler_params=_params("arbitrary"),
        name="final_norm",
    )(h, g.reshape(1, D_MODEL))


def _paired_head_order():
    cols = []
    for j in range(SWA_GROUP):
        cols.extend(range(j * SWA_HEAD_DIM, (j + 1) * SWA_HEAD_DIM))
        cols.extend(range((SWA_GROUP + j) * SWA_HEAD_DIM, (SWA_GROUP + j + 1) * SWA_HEAD_DIM))
    return jnp.asarray(cols, dtype=jnp.int32)


def kernel(x, positions, ln_ffn1, ffn1_w_gate, ffn1_w_up, ffn1_w_down, ln_mix, ln_ffn2, ffn2_w_gate, ffn2_w_up, ffn2_w_down, a_w_qkv, a_w_o, a_lambda_q1, a_lambda_k1, a_lambda_q2, a_lambda_k2, a_subln, b_w_q, b_b_q, b_sinks, b_w_o, b_b_o, kv_norm, kv_w_k, kv_b_k, kv_w_v, kv_b_v, final_norm):
    batch, seq, _ = x.shape
    n = batch * seq
    tables = _rope_tables(positions)
    h = x.reshape(n, D_MODEL)
    perm = _paired_head_order()
    kv = None
    for layer in range(DEPTH):
        h = _ffn(h, ln_ffn1[layer], ffn1_w_gate[layer].astype(BF16), ffn1_w_up[layer].astype(BF16),
                 ffn1_w_down[layer].astype(BF16))
        if layer < N_A_LAYERS:
            a = layer
            lambda_init = 0.8 - 0.6 * math.exp(-0.3 * layer)
            qkv = _proj(h, ln_mix[layer], a_w_qkv[a].astype(BF16), jnp.zeros((3 * D_MODEL,), F32),
                        tables, rope_cols=2 * D_MODEL, scale_cols=D_MODEL,
                        scale=DIFF_SUB_DIM ** -0.5 * LOG2_E, head_major=True)
            lam_vecs = jnp.stack([a_lambda_q1[a], a_lambda_k1[a], a_lambda_q2[a], a_lambda_k2[a]])
            o = _diff_attention(qkv.reshape(3 * DIFF_HEADS, batch, seq, DIFF_V_DIM), lam_vecs,
                                a_subln[a], lambda_init)
            h = _out_proj(h, o.reshape(n, D_MODEL), a_w_o[a].astype(BF16),
                          jnp.zeros((D_MODEL,), F32))
        else:
            b = layer - N_A_LAYERS
            q = _proj(h, ln_mix[layer], b_w_q[b][:, perm].astype(BF16), b_b_q[b][perm], tables,
                      rope_cols=D_MODEL, scale_cols=D_MODEL, scale=SWA_HEAD_DIM ** -0.5)
            sink_pairs = jnp.stack([b_sinks[b][:SWA_GROUP], b_sinks[b][SWA_GROUP:]], axis=1)
            sink_rows = jnp.repeat(sink_pairs.reshape(-1), BLOCK).reshape(-1, 1)
            o = _swa_attention(q.reshape(batch, seq, D_MODEL), kv, sink_rows)
            h = _out_proj(h, o.reshape(n, D_MODEL), b_w_o[b][perm, :].astype(BF16), b_b_o[b])
        h = _ffn(h, ln_ffn2[layer], ffn2_w_gate[layer].astype(BF16), ffn2_w_up[layer].astype(BF16),
                 ffn2_w_down[layer].astype(BF16))
        if layer == N_A_LAYERS - 1:
            w_kv = jnp.concatenate([kv_w_k, kv_w_v], axis=1).astype(BF16)
            b_kv = jnp.concatenate([kv_b_k, kv_b_v])
            kv = _proj(h, kv_norm, w_kv, b_kv, tables, rope_cols=SWA_KV_WIDTH, scale_cols=0,
                       scale=1.0).reshape(batch, seq, 2 * SWA_KV_WIDTH)
    return _final_norm(h, final_norm).reshape(batch, seq, D_MODEL)
```

```python
import functools
import math

import jax
import jax.numpy as jnp
from jax import lax
from jax.experimental import pallas as pl
from jax.experimental.pallas import tpu as pltpu

D_MODEL = 1024
D_FF = 2816
DEPTH = 4
N_A_LAYERS = 2
DIFF_HEADS = 8
DIFF_SUB_DIM = 64
DIFF_V_DIM = 128
SWA_Q_HEADS = 16
SWA_KV_HEADS = 2
SWA_HEAD_DIM = 64
SWA_GROUP = SWA_Q_HEADS // SWA_KV_HEADS
SWA_KV_WIDTH = SWA_KV_HEADS * SWA_HEAD_DIM
WINDOW = 128
BLOCK = 128
ROPE_THETA = 500000.0
ROT_DIM = 16
ROT_HALF = ROT_DIM // 2
NORM_EPS = 1e-5

LANES = 128
VMEM_LIMIT = 56 * 1024 * 1024
MASK_VALUE = -1e30
LOG2_E = math.log2(math.e)

F32 = jnp.float32
BF16 = jnp.bfloat16


def _params(*sem):
    return pltpu.CompilerParams(dimension_semantics=sem, vmem_limit_bytes=VMEM_LIMIT)


def _resident(shape):
    return pl.BlockSpec(shape, lambda *_: (0,) * len(shape), pipeline_mode=pl.Buffered(1))


def _rms_norm(x, g):
    ms = jnp.mean(x * x, axis=-1, keepdims=True)
    return x * lax.rsqrt(ms + NORM_EPS) * g


def _rope_table_kernel(pos_ref, invf_ref, c_ref, s1_ref, s2_ref):
    ang = pos_ref[...] * invf_ref[...]
    cos = jnp.cos(ang)
    sin = jnp.sin(ang)
    lane = lax.broadcasted_iota(jnp.int32, ang.shape, 1) % SWA_HEAD_DIM
    first = lane < ROT_HALF
    second = (lane >= ROT_HALF) & (lane < ROT_DIM)
    c_ref[...] = jnp.where(first | second, cos, 1.0)
    s1_ref[...] = jnp.where(first, -sin, 0.0)
    s2_ref[...] = jnp.where(second, sin, 0.0)


def _rope_tables(positions, tm=1024):
    n = positions.size
    pos = jnp.broadcast_to(positions.reshape(n, 1).astype(F32), (n, LANES))
    inv_freq = ROPE_THETA ** (-jnp.arange(0, ROT_DIM, 2, dtype=F32) / ROT_DIM)
    invf = jnp.tile(inv_freq, LANES // ROT_HALF).reshape(1, LANES)
    spec = pl.BlockSpec((tm, LANES), lambda i: (i, 0))
    out = jax.ShapeDtypeStruct((n, LANES), F32)
    return pl.pallas_call(
        _rope_table_kernel,
        out_shape=(out, out, out),
        grid=(n // tm,),
        in_specs=[spec, pl.BlockSpec((1, LANES), lambda i: (0, 0))],
        out_specs=(spec, spec, spec),
        compiler_params=_params("arbitrary"),
        name="rope_tables",
    )(pos, invf)


def _apply_rope(x, c, s1, s2):
    return (x * c + pltpu.roll(x, LANES - ROT_HALF, 1) * s1 + pltpu.roll(x, ROT_HALF, 1) * s2)


def _ffn_kernel(h_ref, g_ref, wg_ref, wu_ref, wd_ref, o_ref, xn_ref, acc_ref, *, tf):
    xn_ref[...] = _rms_norm(h_ref[...], g_ref[...]).astype(BF16)
    for c in range(D_FF // tf):
        cols = slice(c * tf, (c + 1) * tf)
        xn = xn_ref[...]
        gate = jnp.dot(xn, wg_ref[:, cols], preferred_element_type=F32)
        up = jnp.dot(xn, wu_ref[:, cols], preferred_element_type=F32)
        act = (gate / (1.0 + jnp.exp(-gate)) * up).astype(BF16)
        down = jnp.dot(act, wd_ref[cols, :], preferred_element_type=F32)
        if c == 0:
            acc_ref[...] = down
        else:
            acc_ref[...] += down
    o_ref[...] = h_ref[...] + 0.5 * acc_ref[...]


def _ffn(h, g, wg, wu, wd, *, tm=512, tf=256):
    n = h.shape[0]
    row = pl.BlockSpec((tm, D_MODEL), lambda i: (i, 0))
    return pl.pallas_call(
        functools.partial(_ffn_kernel, tf=tf),
        out_shape=jax.ShapeDtypeStruct((n, D_MODEL), F32),
        grid=(n // tm,),
        in_specs=[row, _resident((1, D_MODEL)), _resident((D_MODEL, D_FF)),
                  _resident((D_MODEL, D_FF)), _resident((D_FF, D_MODEL))],
        out_specs=row,
        scratch_shapes=[pltpu.VMEM((tm, D_MODEL), BF16), pltpu.VMEM((tm, D_MODEL), F32)],
        compiler_params=_params("arbitrary"),
        name="ffn",
    )(h, g.reshape(1, D_MODEL), wg, wu, wd)


def _proj_kernel(h_ref, g_ref, w_ref, b_ref, c_ref, s1_ref, s2_ref, o_ref, xn_ref, *,
                 n_out, rope_cols, scale_cols, scale, chunk, head_major):
    xn_ref[...] = _rms_norm(h_ref[...], g_ref[...]).astype(BF16)
    c, s1, s2 = c_ref[...], s1_ref[...], s2_ref[...]
    for c0 in range(0, n_out, chunk):
        cols = slice(c0, c0 + chunk)
        y = jnp.dot(xn_ref[...], w_ref[:, cols], preferred_element_type=F32) + b_ref[:, cols]
        for j in range(0, chunk, LANES):
            yj = y[:, j:j + LANES]
            if c0 + j < rope_cols:
                yj = _apply_rope(yj, c, s1, s2)
            if c0 + j < scale_cols:
                yj = yj * scale
            if head_major:
                o_ref[(c0 + j) // LANES] = yj.astype(BF16)
            else:
                o_ref[:, c0 + j:c0 + j + LANES] = yj.astype(BF16)


def _proj(h, g, w, b, tables, *, rope_cols, scale_cols, scale, head_major=False, tm=512):
    n = h.shape[0]
    n_out = w.shape[1]
    chunk = min(n_out, 512)
    assert rope_cols % LANES == 0 and scale_cols % LANES == 0 and n_out % chunk == 0
    row = pl.BlockSpec((tm, D_MODEL), lambda i: (i, 0))
    tab = pl.BlockSpec((tm, LANES), lambda i: (i, 0))
    if head_major:
        out_shape = jax.ShapeDtypeStruct((n_out // LANES, n, LANES), BF16)
        out_spec = pl.BlockSpec((n_out // LANES, tm, LANES), lambda i: (0, i, 0))
    else:
        out_shape = jax.ShapeDtypeStruct((n, n_out), BF16)
        out_spec = pl.BlockSpec((tm, n_out), lambda i: (i, 0))
    return pl.pallas_call(
        functools.partial(_proj_kernel, n_out=n_out, rope_cols=rope_cols, scale_cols=scale_cols,
                          scale=scale, chunk=chunk, head_major=head_major),
        out_shape=out_shape,
        grid=(n // tm,),
        in_specs=[row, _resident((1, D_MODEL)), _resident((D_MODEL, n_out)), _resident((1, n_out)),
                  tab, tab, tab],
        out_specs=out_spec,
        scratch_shapes=[pltpu.VMEM((tm, D_MODEL), BF16)],
        compiler_params=_params("arbitrary"),
        name="norm_proj",
    )(h, g.reshape(1, D_MODEL), w, b.reshape(1, n_out), *tables)


def _out_proj_kernel(h_ref, o_ref, w_ref, b_ref, out_ref):
    out_ref[...] = (h_ref[...] + b_ref[...]
                    + jnp.dot(o_ref[...], w_ref[...], preferred_element_type=F32))


def _out_proj(h, o, w, b, *, tm=512):
    n = h.shape[0]
    row = pl.BlockSpec((tm, D_MODEL), lambda i: (i, 0))
    return pl.pallas_call(
        _out_proj_kernel,
        out_shape=jax.ShapeDtypeStruct((n, D_MODEL), F32),
        grid=(n // tm,),
        in_specs=[row, row, _resident((D_MODEL, D_MODEL)), _resident((1, D_MODEL))],
        out_specs=row,
        compiler_params=_params("arbitrary"),
        name="out_proj",
    )(h, o, w, b.reshape(1, D_MODEL))


def _diff_attn_kernel(lam_ref, gsub_ref, q_ref, k_ref, v_ref, o_ref, s_ref, m_ref, acc_ref, *,
                      t, hp, lambda_init):
    i = pl.program_id(2)
    lane = lax.broadcasted_iota(jnp.int32, (t, LANES), 1)
    ones = jnp.ones((t, LANES), BF16)
    q2 = []
    for hh in range(hp):
        q = q_ref[hh, 0]
        zero = jnp.zeros_like(q)
        q2.append(jnp.concatenate([jnp.where(lane < DIFF_SUB_DIM, q, zero),
                                   jnp.where(lane >= DIFF_SUB_DIM, q, zero)], axis=0))

    def scores(hh, j):
        k = k_ref[hh, 0, pl.ds(j * t, t), :]
        return lax.dot_general(q2[hh], k, (((1,), (1,)), ((), ())), preferred_element_type=F32)

    def attend(hh, j, masked):
        s = s_ref[hh]
        if masked:
            row = lax.broadcasted_iota(jnp.int32, s.shape, 0) % t
            col = lax.broadcasted_iota(jnp.int32, s.shape, 1)
            s = jnp.where(col <= row, s, MASK_VALUE)
        m_prev = m_ref[hh]
        m_next = jnp.maximum(m_prev, jnp.max(s, axis=-1, keepdims=True))
        m_ref[hh] = m_next
        alpha = jnp.tile(jnp.exp2(m_prev - m_next), (1, 2))
        p = jnp.exp2(s - jnp.tile(m_next, (1, t // LANES))).astype(BF16)
        v_aug = jnp.concatenate([v_ref[hh, 0, pl.ds(j * t, t), :], ones], axis=1)
        return alpha * acc_ref[hh] + jnp.dot(p, v_aug, preferred_element_type=F32)

    m_ref[...] = jnp.full(m_ref.shape, MASK_VALUE, F32)
    acc_ref[...] = jnp.zeros(acc_ref.shape, F32)
    for hh in range(hp):
        s_ref[hh] = scores(hh, 0)

    def body(j, carry):
        for hh in range(hp):
            acc = attend(hh, j, False)
            s_ref[hh] = scores(hh, j + 1)
            acc_ref[hh] = acc
        return carry

    lax.fori_loop(0, i, body, 0)

    lv = lam_ref[...]
    lam = (jnp.exp(jnp.sum(lv[0:1] * lv[1:2], axis=-1, keepdims=True))
           - jnp.exp(jnp.sum(lv[2:3] * lv[3:4], axis=-1, keepdims=True)) + lambda_init)
    for hh in range(hp):
        acc = attend(hh, i, True)
        attn = acc[:, :DIFF_V_DIM] / acc[:, DIFF_V_DIM:]
        o = attn[:t] - lam * attn[t:]
        o = _rms_norm(o, gsub_ref[...]) * (1.0 - lambda_init)
        o_ref[0, :, hh * DIFF_V_DIM:(hh + 1) * DIFF_V_DIM] = o.astype(BF16)


def _diff_attention(qkv, lam_vecs, g_sub, lambda_init, *, t=512, hp=2):
    _, b, s, _ = qkv.shape
    hg = DIFF_HEADS // hp
    return pl.pallas_call(
        functools.partial(_diff_attn_kernel, t=t, hp=hp, lambda_init=lambda_init),
        out_shape=jax.ShapeDtypeStruct((b, s, D_MODEL), BF16),
        grid=(b, hg, s // t),
        in_specs=[
            pl.BlockSpec((4, DIFF_SUB_DIM), lambda bi, hi, i: (0, 0)),
            pl.BlockSpec((1, DIFF_V_DIM), lambda bi, hi, i: (0, 0)),
            pl.BlockSpec((hp, 1, t, DIFF_V_DIM), lambda bi, hi, i: (hi, bi, i, 0)),
            pl.BlockSpec((hp, 1, s, DIFF_V_DIM), lambda bi, hi, i: (hg + hi, bi, 0, 0)),
            pl.BlockSpec((hp, 1, s, DIFF_V_DIM), lambda bi, hi, i: (2 * hg + hi, bi, 0, 0)),
        ],
        out_specs=pl.BlockSpec((1, t, hp * DIFF_V_DIM), lambda bi, hi, i: (bi, i, hi)),
        scratch_shapes=[pltpu.VMEM((hp, 2 * t, t), F32), pltpu.VMEM((hp, 2 * t, LANES), F32),
                        pltpu.VMEM((hp, 2 * t, 2 * DIFF_V_DIM), F32)],
        compiler_params=_params("arbitrary", "arbitrary", "arbitrary"),
        name="diff_attention",
    )(lam_vecs, g_sub.reshape(1, DIFF_V_DIM), qkv, qkv, qkv)


def _swa_kernel(sink_ref, q_ref, kp_ref, kc_ref, vp_ref, vc_ref, o_ref, *, tq):
    i = pl.program_id(1)
    kk = jnp.concatenate([kp_ref[0], kc_ref[0]], axis=0)
    vv = jnp.concatenate([vp_ref[0], vc_ref[0]], axis=0)
    nj = D_MODEL // LANES
    rows = 2 * nj * BLOCK
    lane_q = lax.broadcasted_iota(jnp.int32, (BLOCK, LANES), 1)
    qi = lax.broadcasted_iota(jnp.int32, (rows, 2 * BLOCK), 0) % BLOCK
    kj = lax.broadcasted_iota(jnp.int32, (rows, 2 * BLOCK), 1)
    in_window = (kj > qi) & (kj <= qi + WINDOW)
    sink = sink_ref[...]
    for c in range(tq // BLOCK):
        q = q_ref[0, c * BLOCK:(c + 1) * BLOCK, :]
        parts = []
        for j in range(nj):
            qj = q[:, j * LANES:(j + 1) * LANES]
            parts.append(jnp.where(lane_q < SWA_HEAD_DIM, qj, jnp.zeros_like(qj)))
            parts.append(jnp.where(lane_q >= SWA_HEAD_DIM, qj, jnp.zeros_like(qj)))
        q2 = jnp.concatenate(parts, axis=0)
        k = kk[c * BLOCK:(c + 2) * BLOCK]
        v = vv[c * BLOCK:(c + 2) * BLOCK]
        s = lax.dot_general(q2, k, (((1,), (1,)), ((), ())), preferred_element_type=F32)
        allowed = in_window
        if c == 0:
            allowed = allowed & (kj >= jnp.where(i > 0, 0, BLOCK))
        s = jnp.where(allowed, s, MASK_VALUE)
        m = jnp.maximum(jnp.max(s, axis=-1, keepdims=True), sink)
        p = jnp.exp(s - m)
        denom = jnp.sum(p, axis=-1, keepdims=True) + jnp.exp(sink - m)
        o2 = jnp.dot(p.astype(BF16), v, preferred_element_type=F32) / denom
        outs = []
        for j in range(nj):
            lo = o2[(2 * j) * BLOCK:(2 * j + 1) * BLOCK]
            hi = o2[(2 * j + 1) * BLOCK:(2 * j + 2) * BLOCK]
            outs.append(jnp.where(lane_q < SWA_HEAD_DIM, lo, hi))
        o_ref[0, c * BLOCK:(c + 1) * BLOCK, :] = jnp.concatenate(outs, axis=1).astype(BF16)


def _swa_attention(q, kv, sink_rows, *, tq=512):
    b, s, _ = q.shape
    per = tq // BLOCK
    cur = lambda bi, i: (bi, i, 0)
    prev = lambda bi, i: (bi, jnp.maximum(i * per - 1, 0), 0)
    cur_v = lambda bi, i: (bi, i, 1)
    prev_v = lambda bi, i: (bi, jnp.maximum(i * per - 1, 0), 1)
    rows = sink_rows.shape[0]
    return pl.pallas_call(
        functools.partial(_swa_kernel, tq=tq),
        out_shape=jax.ShapeDtypeStruct((b, s, D_MODEL), BF16),
        grid=(b, s // tq),
        in_specs=[
            pl.BlockSpec((rows, 1), lambda bi, i: (0, 0)),
            pl.BlockSpec((1, tq, D_MODEL), cur),
            pl.BlockSpec((1, BLOCK, LANES), prev),
            pl.BlockSpec((1, tq, LANES), cur),
            pl.BlockSpec((1, BLOCK, LANES), prev_v),
            pl.BlockSpec((1, tq, LANES), cur_v),
        ],
        out_specs=pl.BlockSpec((1, tq, D_MODEL), cur),
        compiler_params=_params("arbitrary", "arbitrary"),
        name="swa_attention",
    )(sink_rows, q, kv, kv, kv, kv)


def _final_norm_kernel(h_ref, g_ref, o_ref):
    o_ref[...] = _rms_norm(h_ref[...], g_ref[...])


def _final_norm(h, g, *, tm=1024):
    n = h.shape[0]
    row = pl.BlockSpec((tm, D_MODEL), lambda i: (i, 0))
    return pl.pallas_call(
        _final_norm_kernel,
        out_shape=jax.ShapeDtypeStruct((n, D_MODEL), F32),
        grid=(n // tm,),
        in_specs=[row, pl.BlockSpec((1, D_MODEL), lambda i: (0, 0))],
        out_specs=row,
        compiler_params=_params("arbitrary"),
        name="final_norm",
    )(h, g.reshape(1, D_MODEL))


def _paired_head_order():
    cols = []
    for j in range(SWA_GROUP):
        cols.extend(range(j * SWA_HEAD_DIM, (j + 1) * SWA_HEAD_DIM))
        cols.extend(range((SWA_GROUP + j) * SWA_HEAD_DIM, (SWA_GROUP + j + 1) * SWA_HEAD_DIM))
    return jnp.asarray(cols, dtype=jnp.int32)


def kernel(x, positions, ln_ffn1, ffn1_w_gate, ffn1_w_up, ffn1_w_down, ln_mix, ln_ffn2, ffn2_w_gate, ffn2_w_up, ffn2_w_down, a_w_qkv, a_w_o, a_lambda_q1, a_lambda_k1, a_lambda_q2, a_lambda_k2, a_subln, b_w_q, b_b_q, b_sinks, b_w_o, b_b_o, kv_norm, kv_w_k, kv_b_k, kv_w_v, kv_b_v, final_norm):
    batch, seq, _ = x.shape
    n = batch * seq
    tables = _rope_tables(positions)
    h = x.reshape(n, D_MODEL)
    perm = _paired_head_order()
    kv = None
    for layer in range(DEPTH):
        h = _ffn(h, ln_ffn1[layer], ffn1_w_gate[layer].astype(BF16), ffn1_w_up[layer].astype(BF16),
                 ffn1_w_down[layer].astype(BF16))
        if layer < N_A_LAYERS:
            a = layer
            lambda_init = 0.8 - 0.6 * math.exp(-0.3 * layer)
            qkv = _proj(h, ln_mix[layer], a_w_qkv[a].astype(BF16), jnp.zeros((3 * D_MODEL,), F32),
                        tables, rope_cols=2 * D_MODEL, scale_cols=D_MODEL,
                        scale=DIFF_SUB_DIM ** -0.5 * LOG2_E, head_major=True)
            lam_vecs = jnp.stack([a_lambda_q1[a], a_lambda_k1[a], a_lambda_q2[a], a_lambda_k2[a]])
            o = _diff_attention(qkv.reshape(3 * DIFF_HEADS, batch, seq, DIFF_V_DIM), lam_vecs,
                                a_subln[a], lambda_init)
            h = _out_proj(h, o.reshape(n, D_MODEL), a_w_o[a].astype(BF16),
                          jnp.zeros((D_MODEL,), F32))
        else:
            b = layer - N_A_LAYERS
            q = _proj(h, ln_mix[layer], b_w_q[b][:, perm].astype(BF16), b_b_q[b][perm], tables,
                      rope_cols=D_MODEL, scale_cols=D_MODEL, scale=SWA_HEAD_DIM ** -0.5)
            sink_pairs = jnp.stack([b_sinks[b][:SWA_GROUP], b_sinks[b][SWA_GROUP:]], axis=1)
            sink_rows = jnp.repeat(sink_pairs.reshape(-1), BLOCK).reshape(-1, 1)
            o = _swa_attention(q.reshape(batch, seq, D_MODEL), kv, sink_rows)
            h = _out_proj(h, o.reshape(n, D_MODEL), b_w_o[b][perm, :].astype(BF16), b_b_o[b])
        h = _ffn(h, ln_ffn2[layer], ffn2_w_gate[layer].astype(BF16), ffn2_w_up[layer].astype(BF16),
                 ffn2_w_down[layer].astype(BF16))
        if layer == N_A_LAYERS - 1:
            w_kv = jnp.concatenate([kv_w_k, kv_w_v], axis=1).astype(BF16)
            b_kv = jnp.concatenate([kv_b_k, kv_b_v])
            kv = _proj(h, kv_norm, w_kv, b_kv, tables, rope_cols=SWA_KV_WIDTH, scale_cols=0,
                       scale=1.0).reshape(batch, seq, 2 * SWA_KV_WIDTH)
    return _final_norm(h, final_norm).reshape(batch, seq, D_MODEL)
```

```python
import functools
import math

import jax
import jax.numpy as jnp
from jax import lax
from jax.experimental import pallas as pl
from jax.experimental.pallas import tpu as pltpu

D_MODEL = 1024
D_FF = 2816
DEPTH = 4
N_A_LAYERS = 2
DIFF_HEADS = 8
DIFF_SUB_DIM = 64
DIFF_V_DIM = 128
SWA_Q_HEADS = 16
SWA_KV_HEADS = 2
SWA_HEAD_DIM = 64
SWA_GROUP = SWA_Q_HEADS // SWA_KV_HEADS
SWA_KV_WIDTH = SWA_KV_HEADS * SWA_HEAD_DIM
WINDOW = 128
BLOCK = 128
ROPE_THETA = 500000.0
ROT_DIM = 16
ROT_HALF = ROT_DIM // 2
NORM_EPS = 1e-5

LANES = 128
VMEM_LIMIT = 56 * 1024 * 1024
MASK_VALUE = -1e30
LOG2_E = math.log2(math.e)

F32 = jnp.float32
BF16 = jnp.bfloat16


def _params(*sem):
    return pltpu.CompilerParams(dimension_semantics=sem, vmem_limit_bytes=VMEM_LIMIT)


def _resident(shape):
    return pl.BlockSpec(shape, lambda *_: (0,) * len(shape), pipeline_mode=pl.Buffered(1))


def _rms_norm(x, g):
    ms = jnp.mean(x * x, axis=-1, keepdims=True)
    return x * lax.rsqrt(ms + NORM_EPS) * g


def _rope_table_kernel(pos_ref, invf_ref, c_ref, s1_ref, s2_ref):
    ang = pos_ref[...] * invf_ref[...]
    cos = jnp.cos(ang)
    sin = jnp.sin(ang)
    lane = lax.broadcasted_iota(jnp.int32, ang.shape, 1) % SWA_HEAD_DIM
    first = lane < ROT_HALF
    second = (lane >= ROT_HALF) & (lane < ROT_DIM)
    c_ref[...] = jnp.where(first | second, cos, 1.0)
    s1_ref[...] = jnp.where(first, -sin, 0.0)
    s2_ref[...] = jnp.where(second, sin, 0.0)


def _rope_tables(positions, tm=1024):
    n = positions.size
    pos = jnp.broadcast_to(positions.reshape(n, 1).astype(F32), (n, LANES))
    inv_freq = ROPE_THETA ** (-jnp.arange(0, ROT_DIM, 2, dtype=F32) / ROT_DIM)
    invf = jnp.tile(inv_freq, LANES // ROT_HALF).reshape(1, LANES)
    spec = pl.BlockSpec((tm, LANES), lambda i: (i, 0))
    out = jax.ShapeDtypeStruct((n, LANES), F32)
    return pl.pallas_call(
        _rope_table_kernel,
        out_shape=(out, out, out),
        grid=(n // tm,),
        in_specs=[spec, pl.BlockSpec((1, LANES), lambda i: (0, 0))],
        out_specs=(spec, spec, spec),
        compiler_params=_params("arbitrary"),
        name="rope_tables",
    )(pos, invf)


def _apply_rope(x, c, s1, s2):
    return (x * c + pltpu.roll(x, LANES - ROT_HALF, 1) * s1 + pltpu.roll(x, ROT_HALF, 1) * s2)


def _ffn_kernel(*refs, tf, has_attn, has_final):
    h_ref, g_ref, wg_ref, wu_ref, wd_ref = refs[:5]
    rest = list(refs[5:])
    if has_attn:
        a_ref, wo_ref, bo_ref = rest[:3]
        rest = rest[3:]
    if has_final:
        gf_ref = rest.pop(0)
    o_ref, xn_ref, acc_ref = rest[:3]
    h = h_ref[...]
    if has_attn:
        res_ref = rest[3]
        h = h + bo_ref[...] + jnp.dot(a_ref[...], wo_ref[...], preferred_element_type=F32)
        res_ref[...] = h
    else:
        res_ref = h_ref
    xn_ref[...] = _rms_norm(h, g_ref[...]).astype(BF16)
    for c in range(D_FF // tf):
        cols = slice(c * tf, (c + 1) * tf)
        xn = xn_ref[...]
        gate = jnp.dot(xn, wg_ref[:, cols], preferred_element_type=F32)
        up = jnp.dot(xn, wu_ref[:, cols], preferred_element_type=F32)
        act = (gate / (1.0 + jnp.exp(-gate)) * up).astype(BF16)
        down = jnp.dot(act, wd_ref[cols, :], preferred_element_type=F32)
        if c == 0:
            acc_ref[...] = down
        else:
            acc_ref[...] += down
    out = res_ref[...] + 0.5 * acc_ref[...]
    if has_final:
        out = _rms_norm(out, gf_ref[...])
    o_ref[...] = out


def _ffn(h, g, wg, wu, wd, *, attn=None, final_g=None, tm=512, tf=256):
    n = h.shape[0]
    row = pl.BlockSpec((tm, D_MODEL), lambda i: (i, 0))
    vec = _resident((1, D_MODEL))
    args = [h, g.reshape(1, D_MODEL), wg, wu, wd]
    specs = [row, vec, _resident((D_MODEL, D_FF)), _resident((D_MODEL, D_FF)),
             _resident((D_FF, D_MODEL))]
    scratch = [pltpu.VMEM((tm, D_MODEL), BF16), pltpu.VMEM((tm, D_MODEL), F32)]
    if attn is not None:
        a, wo, bo = attn
        args += [a, wo, bo.reshape(1, D_MODEL)]
        specs += [row, _resident((D_MODEL, D_MODEL)), vec]
        scratch.append(pltpu.VMEM((tm, D_MODEL), F32))
    if final_g is not None:
        args.append(final_g.reshape(1, D_MODEL))
        specs.append(vec)
    return pl.pallas_call(
        functools.partial(_ffn_kernel, tf=tf, has_attn=attn is not None,
                          has_final=final_g is not None),
        out_shape=jax.ShapeDtypeStruct((n, D_MODEL), F32),
        grid=(n // tm,),
        in_specs=specs,
        out_specs=row,
        scratch_shapes=scratch,
        compiler_params=_params("arbitrary"),
        name="ffn",
    )(*args)


def _proj_kernel(h_ref, g_ref, w_ref, b_ref, c_ref, s1_ref, s2_ref, o_ref, xn_ref, *,
                 n_out, rope_cols, scale_cols, scale, chunk, head_major):
    xn_ref[...] = _rms_norm(h_ref[...], g_ref[...]).astype(BF16)
    c, s1, s2 = c_ref[...], s1_ref[...], s2_ref[...]
    for c0 in range(0, n_out, chunk):
        cols = slice(c0, c0 + chunk)
        y = jnp.dot(xn_ref[...], w_ref[:, cols], preferred_element_type=F32) + b_ref[:, cols]
        for j in range(0, chunk, LANES):
            yj = y[:, j:j + LANES]
            if c0 + j < rope_cols:
                yj = _apply_rope(yj, c, s1, s2)
            if c0 + j < scale_cols:
                yj = yj * scale
            if head_major:
                o_ref[(c0 + j) // LANES] = yj.astype(BF16)
            else:
                o_ref[:, c0 + j:c0 + j + LANES] = yj.astype(BF16)


def _proj(h, g, w, b, tables, *, rope_cols, scale_cols, scale, head_major=False, tm=512):
    n = h.shape[0]
    n_out = w.shape[1]
    chunk = min(n_out, 512)
    assert rope_cols % LANES == 0 and scale_cols % LANES == 0 and n_out % chunk == 0
    row = pl.BlockSpec((tm, D_MODEL), lambda i: (i, 0))
    tab = pl.BlockSpec((tm, LANES), lambda i: (i, 0))
    if head_major:
        out_shape = jax.ShapeDtypeStruct((n_out // LANES, n, LANES), BF16)
        out_spec = pl.BlockSpec((n_out // LANES, tm, LANES), lambda i: (0, i, 0))
    else:
        out_shape = jax.ShapeDtypeStruct((n, n_out), BF16)
        out_spec = pl.BlockSpec((tm, n_out), lambda i: (i, 0))
    return pl.pallas_call(
        functools.partial(_proj_kernel, n_out=n_out, rope_cols=rope_cols, scale_cols=scale_cols,
                          scale=scale, chunk=chunk, head_major=head_major),
        out_shape=out_shape,
        grid=(n // tm,),
        in_specs=[row, _resident((1, D_MODEL)), _resident((D_MODEL, n_out)), _resident((1, n_out)),
                  tab, tab, tab],
        out_specs=out_spec,
        scratch_shapes=[pltpu.VMEM((tm, D_MODEL), BF16)],
        compiler_params=_params("arbitrary"),
        name="norm_proj",
    )(h, g.reshape(1, D_MODEL), w, b.reshape(1, n_out), *tables)


def _diff_attn_kernel(lam_ref, gsub_ref, q_ref, k_ref, v_ref, o_ref, s_ref, m_ref, acc_ref, *,
                      t, hp, lambda_init):
    i = pl.program_id(2)
    lane = lax.broadcasted_iota(jnp.int32, (t, LANES), 1)
    ones = jnp.ones((t, LANES), BF16)
    q2 = []
    for hh in range(hp):
        q = q_ref[hh, 0]
        zero = jnp.zeros_like(q)
        q2.append(jnp.concatenate([jnp.where(lane < DIFF_SUB_DIM, q, zero),
                                   jnp.where(lane >= DIFF_SUB_DIM, q, zero)], axis=0))

    def scores(hh, j):
        k = k_ref[hh, 0, pl.ds(j * t, t), :]
        return lax.dot_general(q2[hh], k, (((1,), (1,)), ((), ())), preferred_element_type=F32)

    def attend(hh, j, masked):
        s = s_ref[hh]
        if masked:
            row = lax.broadcasted_iota(jnp.int32, s.shape, 0) % t
            col = lax.broadcasted_iota(jnp.int32, s.shape, 1)
            s = jnp.where(col <= row, s, MASK_VALUE)
        m_prev = m_ref[hh]
        m_next = jnp.maximum(m_prev, jnp.max(s, axis=-1, keepdims=True))
        m_ref[hh] = m_next
        alpha = jnp.tile(jnp.exp2(m_prev - m_next), (1, 2))
        p = jnp.exp2(s - jnp.tile(m_next, (1, t // LANES))).astype(BF16)
        v_aug = jnp.concatenate([v_ref[hh, 0, pl.ds(j * t, t), :], ones], axis=1)
        return alpha * acc_ref[hh] + jnp.dot(p, v_aug, preferred_element_type=F32)

    m_ref[...] = jnp.full(m_ref.shape, MASK_VALUE, F32)
    acc_ref[...] = jnp.zeros(acc_ref.shape, F32)
    for hh in range(hp):
        s_ref[hh] = scores(hh, 0)

    def body(j, carry):
        for hh in range(hp):
            acc = attend(hh, j, False)
            s_ref[hh] = scores(hh, j + 1)
            acc_ref[hh] = acc
        return carry

    lax.fori_loop(0, i, body, 0)

    lv = lam_ref[...]
    lam = (jnp.exp(jnp.sum(lv[0:1] * lv[1:2], axis=-1, keepdims=True))
           - jnp.exp(jnp.sum(lv[2:3] * lv[3:4], axis=-1, keepdims=True)) + lambda_init)
    for hh in range(hp):
        acc = attend(hh, i, True)
        attn = acc[:, :DIFF_V_DIM] / acc[:, DIFF_V_DIM:]
        o = attn[:t] - lam * attn[t:]
        o = _rms_norm(o, gsub_ref[...]) * (1.0 - lambda_init)
        o_ref[0, :, hh * DIFF_V_DIM:(hh + 1) * DIFF_V_DIM] = o.astype(BF16)


def _diff_attention(qkv, lam_vecs, g_sub, lambda_init, *, t=512, hp=2):
    _, b, s, _ = qkv.shape
    hg = DIFF_HEADS // hp
    return pl.pallas_call(
        functools.partial(_diff_attn_kernel, t=t, hp=hp, lambda_init=lambda_init),
        out_shape=jax.ShapeDtypeStruct((b, s, D_MODEL), BF16),
        grid=(b, hg, s // t),
        in_specs=[
            pl.BlockSpec((4, DIFF_SUB_DIM), lambda bi, hi, i: (0, 0)),
            pl.BlockSpec((1, DIFF_V_DIM), lambda bi, hi, i: (0, 0)),
            pl.BlockSpec((hp, 1, t, DIFF_V_DIM), lambda bi, hi, i: (hi, bi, i, 0)),
            pl.BlockSpec((hp, 1, s, DIFF_V_DIM), lambda bi, hi, i: (hg + hi, bi, 0, 0)),
            pl.BlockSpec((hp, 1, s, DIFF_V_DIM), lambda bi, hi, i: (2 * hg + hi, bi, 0, 0)),
        ],
        out_specs=pl.BlockSpec((1, t, hp * DIFF_V_DIM), lambda bi, hi, i: (bi, i, hi)),
        scratch_shapes=[pltpu.VMEM((hp, 2 * t, t), F32), pltpu.VMEM((hp, 2 * t, LANES), F32),
                        pltpu.VMEM((hp, 2 * t, 2 * DIFF_V_DIM), F32)],
        compiler_params=_params("arbitrary", "arbitrary", "arbitrary"),
        name="diff_attention",
    )(lam_vecs, g_sub.reshape(1, DIFF_V_DIM), qkv, qkv, qkv)


def _swa_kernel(sink_ref, q_ref, kp_ref, kc_ref, vp_ref, vc_ref, o_ref, *, tq):
    i = pl.program_id(1)
    kk = jnp.concatenate([kp_ref[0], kc_ref[0]], axis=0)
    vv = jnp.concatenate([vp_ref[0], vc_ref[0]], axis=0)
    nj = D_MODEL // LANES
    rows = 2 * nj * BLOCK
    lane_q = lax.broadcasted_iota(jnp.int32, (BLOCK, LANES), 1)
    qi = lax.broadcasted_iota(jnp.int32, (rows, 2 * BLOCK), 0) % BLOCK
    kj = lax.broadcasted_iota(jnp.int32, (rows, 2 * BLOCK), 1)
    in_window = (kj > qi) & (kj <= qi + WINDOW)
    sink = sink_ref[...]
    for c in range(tq // BLOCK):
        q = q_ref[0, c * BLOCK:(c + 1) * BLOCK, :]
        parts = []
        for j in range(nj):
            qj = q[:, j * LANES:(j + 1) * LANES]
            parts.append(jnp.where(lane_q < SWA_HEAD_DIM, qj, jnp.zeros_like(qj)))
            parts.append(jnp.where(lane_q >= SWA_HEAD_DIM, qj, jnp.zeros_like(qj)))
        q2 = jnp.concatenate(parts, axis=0)
        k = kk[c * BLOCK:(c + 2) * BLOCK]
        v = vv[c * BLOCK:(c + 2) * BLOCK]
        s = lax.dot_general(q2, k, (((1,), (1,)), ((), ())), preferred_element_type=F32)
        allowed = in_window
        if c == 0:
            allowed = allowed & (kj >= jnp.where(i > 0, 0, BLOCK))
        s = jnp.where(allowed, s, MASK_VALUE)
        m = jnp.maximum(jnp.max(s, axis=-1, keepdims=True), sink)
        p = jnp.exp(s - m)
        denom = jnp.sum(p, axis=-1, keepdims=True) + jnp.exp(sink - m)
        o2 = jnp.dot(p.astype(BF16), v, preferred_element_type=F32) / denom
        outs = []
        for j in range(nj):
            lo = o2[(2 * j) * BLOCK:(2 * j + 1) * BLOCK]
            hi = o2[(2 * j + 1) * BLOCK:(2 * j + 2) * BLOCK]
            outs.append(jnp.where(lane_q < SWA_HEAD_DIM, lo, hi))
        o_ref[0, c * BLOCK:(c + 1) * BLOCK, :] = jnp.concatenate(outs, axis=1).astype(BF16)


def _swa_attention(q, kv, sink_rows, *, tq=512):
    b, s, _ = q.shape
    per = tq // BLOCK
    cur = lambda bi, i: (bi, i, 0)
    prev = lambda bi, i: (bi, jnp.maximum(i * per - 1, 0), 0)
    cur_v = lambda bi, i: (bi, i, 1)
    prev_v = lambda bi, i: (bi, jnp.maximum(i * per - 1, 0), 1)
    rows = sink_rows.shape[0]
    return pl.pallas_call(
        functools.partial(_swa_kernel, tq=tq),
        out_shape=jax.ShapeDtypeStruct((b, s, D_MODEL), BF16),
        grid=(b, s // tq),
        in_specs=[
            pl.BlockSpec((rows, 1), lambda bi, i: (0, 0)),
            pl.BlockSpec((1, tq, D_MODEL), cur),
            pl.BlockSpec((1, BLOCK, LANES), prev),
            pl.BlockSpec((1, tq, LANES), cur),
            pl.BlockSpec((1, BLOCK, LANES), prev_v),
            pl.BlockSpec((1, tq, LANES), cur_v),
        ],
        out_specs=pl.BlockSpec((1, tq, D_MODEL), cur),
        compiler_params=_params("arbitrary", "arbitrary"),
        name="swa_attention",
    )(sink_rows, q, kv, kv, kv, kv)


def _paired_head_order():
    cols = []
    for j in range(SWA_GROUP):
        cols.extend(range(j * SWA_HEAD_DIM, (j + 1) * SWA_HEAD_DIM))
        cols.extend(range((SWA_GROUP + j) * SWA_HEAD_DIM, (SWA_GROUP + j + 1) * SWA_HEAD_DIM))
    return jnp.asarray(cols, dtype=jnp.int32)


def kernel(x, positions, ln_ffn1, ffn1_w_gate, ffn1_w_up, ffn1_w_down, ln_mix, ln_ffn2, ffn2_w_gate, ffn2_w_up, ffn2_w_down, a_w_qkv, a_w_o, a_lambda_q1, a_lambda_k1, a_lambda_q2, a_lambda_k2, a_subln, b_w_q, b_b_q, b_sinks, b_w_o, b_b_o, kv_norm, kv_w_k, kv_b_k, kv_w_v, kv_b_v, final_norm):
    batch, seq, _ = x.shape
    n = batch * seq
    tables = _rope_tables(positions)
    h = x.reshape(n, D_MODEL)
    perm = _paired_head_order()
    kv = None
    for layer in range(DEPTH):
        h = _ffn(h, ln_ffn1[layer], ffn1_w_gate[layer].astype(BF16), ffn1_w_up[layer].astype(BF16),
                 ffn1_w_down[layer].astype(BF16))
        if layer < N_A_LAYERS:
            a = layer
            lambda_init = 0.8 - 0.6 * math.exp(-0.3 * layer)
            qkv = _proj(h, ln_mix[layer], a_w_qkv[a].astype(BF16), jnp.zeros((3 * D_MODEL,), F32),
                        tables, rope_cols=2 * D_MODEL, scale_cols=D_MODEL,
                        scale=DIFF_SUB_DIM ** -0.5 * LOG2_E, head_major=True)
            lam_vecs = jnp.stack([a_lambda_q1[a], a_lambda_k1[a], a_lambda_q2[a], a_lambda_k2[a]])
            o = _diff_attention(qkv.reshape(3 * DIFF_HEADS, batch, seq, DIFF_V_DIM), lam_vecs,
                                a_subln[a], lambda_init)
            attn = (o.reshape(n, D_MODEL), a_w_o[a].astype(BF16), jnp.zeros((D_MODEL,), F32))
        else:
            b = layer - N_A_LAYERS
            q = _proj(h, ln_mix[layer], b_w_q[b][:, perm].astype(BF16), b_b_q[b][perm], tables,
                      rope_cols=D_MODEL, scale_cols=D_MODEL, scale=SWA_HEAD_DIM ** -0.5)
            sink_pairs = jnp.stack([b_sinks[b][:SWA_GROUP], b_sinks[b][SWA_GROUP:]], axis=1)
            sink_rows = jnp.repeat(sink_pairs.reshape(-1), BLOCK).reshape(-1, 1)
            o = _swa_attention(q.reshape(batch, seq, D_MODEL), kv, sink_rows)
            attn = (o.reshape(n, D_MODEL), b_w_o[b][perm, :].astype(BF16), b_b_o[b])
        h = _ffn(h, ln_ffn2[layer], ffn2_w_gate[layer].astype(BF16), ffn2_w_up[layer].astype(BF16),
                 ffn2_w_down[layer].astype(BF16), attn=attn,
                 final_g=final_norm if layer == DEPTH - 1 else None)
        if layer == N_A_LAYERS - 1:
            w_kv = jnp.concatenate([kv_w_k, kv_w_v], axis=1).astype(BF16)
            b_kv = jnp.concatenate([kv_b_k, kv_b_v])
            kv = _proj(h, kv_norm, w_kv, b_kv, tables, rope_cols=SWA_KV_WIDTH, scale_cols=0,
                       scale=1.0).reshape(batch, seq, 2 * SWA_KV_WIDTH)
    return h.reshape(batch, seq, D_MODEL)
```

```python
import functools
import math

import jax
import jax.numpy as jnp
from jax import lax
from jax.experimental import pallas as pl
from jax.experimental.pallas import tpu as pltpu

D_MODEL = 1024
D_FF = 2816
DEPTH = 4
N_A_LAYERS = 2
DIFF_HEADS = 8
DIFF_SUB_DIM = 64
DIFF_V_DIM = 128
SWA_Q_HEADS = 16
SWA_KV_HEADS = 2
SWA_HEAD_DIM = 64
SWA_GROUP = SWA_Q_HEADS // SWA_KV_HEADS
SWA_KV_WIDTH = SWA_KV_HEADS * SWA_HEAD_DIM
WINDOW = 128
BLOCK = 128
ROPE_THETA = 500000.0
ROT_DIM = 16
ROT_HALF = ROT_DIM // 2
NORM_EPS = 1e-5

LANES = 128
VMEM_LIMIT = 56 * 1024 * 1024
MASK_VALUE = -1e30
LOG2_E = math.log2(math.e)

F32 = jnp.float32
BF16 = jnp.bfloat16


def _params(*sem):
    return pltpu.CompilerParams(dimension_semantics=sem, vmem_limit_bytes=VMEM_LIMIT)


def _resident(shape):
    return pl.BlockSpec(shape, lambda *_: (0,) * len(shape), pipeline_mode=pl.Buffered(1))


def _rms_norm(x, g):
    ms = jnp.mean(x * x, axis=-1, keepdims=True)
    return x * lax.rsqrt(ms + NORM_EPS) * g


def _rope_table_kernel(pos_ref, invf_ref, c_ref, s1_ref, s2_ref):
    ang = pos_ref[...] * invf_ref[...]
    cos = jnp.cos(ang)
    sin = jnp.sin(ang)
    lane = lax.broadcasted_iota(jnp.int32, ang.shape, 1) % SWA_HEAD_DIM
    first = lane < ROT_HALF
    second = (lane >= ROT_HALF) & (lane < ROT_DIM)
    c_ref[...] = jnp.where(first | second, cos, 1.0)
    s1_ref[...] = jnp.where(first, -sin, 0.0)
    s2_ref[...] = jnp.where(second, sin, 0.0)


def _rope_tables(positions, tm=1024):
    n = positions.size
    pos = jnp.broadcast_to(positions.reshape(n, 1).astype(F32), (n, LANES))
    inv_freq = ROPE_THETA ** (-jnp.arange(0, ROT_DIM, 2, dtype=F32) / ROT_DIM)
    invf = jnp.tile(inv_freq, LANES // ROT_HALF).reshape(1, LANES)
    spec = pl.BlockSpec((tm, LANES), lambda i: (i, 0))
    out = jax.ShapeDtypeStruct((n, LANES), F32)
    return pl.pallas_call(
        _rope_table_kernel,
        out_shape=(out, out, out),
        grid=(n // tm,),
        in_specs=[spec, pl.BlockSpec((1, LANES), lambda i: (0, 0))],
        out_specs=(spec, spec, spec),
        compiler_params=_params("arbitrary"),
        name="rope_tables",
    )(pos, invf)


def _apply_rope(x, c, s1, s2):
    return (x * c + pltpu.roll(x, LANES - ROT_HALF, 1) * s1 + pltpu.roll(x, ROT_HALF, 1) * s2)


def _ffn_kernel(*refs, tf, has_attn, has_final):
    h_ref, g_ref, wg_ref, wu_ref, wd_ref = refs[:5]
    rest = list(refs[5:])
    if has_attn:
        a_ref, wo_ref, bo_ref = rest[:3]
        rest = rest[3:]
    if has_final:
        gf_ref = rest.pop(0)
    o_ref, xn_ref, acc_ref = rest[:3]
    h = h_ref[...]
    if has_attn:
        res_ref = rest[3]
        h = h + bo_ref[...] + jnp.dot(a_ref[...], wo_ref[...], preferred_element_type=F32)
        res_ref[...] = h
    else:
        res_ref = h_ref
    xn_ref[...] = _rms_norm(h, g_ref[...]).astype(BF16)
    for c in range(D_FF // tf):
        cols = slice(c * tf, (c + 1) * tf)
        xn = xn_ref[...]
        gate = jnp.dot(xn, wg_ref[:, cols], preferred_element_type=F32)
        up = jnp.dot(xn, wu_ref[:, cols], preferred_element_type=F32)
        act = (gate / (1.0 + jnp.exp(-gate)) * up).astype(BF16)
        down = jnp.dot(act, wd_ref[cols, :], preferred_element_type=F32)
        if c == 0:
            acc_ref[...] = down
        else:
            acc_ref[...] += down
    out = res_ref[...] + 0.5 * acc_ref[...]
    if has_final:
        out = _rms_norm(out, gf_ref[...])
    o_ref[...] = out


def _ffn(h, g, wg, wu, wd, *, attn=None, final_g=None, tm=512, tf=256):
    n = h.shape[0]
    row = pl.BlockSpec((tm, D_MODEL), lambda i: (i, 0))
    vec = _resident((1, D_MODEL))
    args = [h, g.reshape(1, D_MODEL), wg, wu, wd]
    specs = [row, vec, _resident((D_MODEL, D_FF)), _resident((D_MODEL, D_FF)),
             _resident((D_FF, D_MODEL))]
    scratch = [pltpu.VMEM((tm, D_MODEL), BF16), pltpu.VMEM((tm, D_MODEL), F32)]
    if attn is not None:
        a, wo, bo = attn
        args += [a, wo, bo.reshape(1, D_MODEL)]
        specs += [row, _resident((D_MODEL, D_MODEL)), vec]
        scratch.append(pltpu.VMEM((tm, D_MODEL), F32))
    if final_g is not None:
        args.append(final_g.reshape(1, D_MODEL))
        specs.append(vec)
    return pl.pallas_call(
        functools.partial(_ffn_kernel, tf=tf, has_attn=attn is not None,
                          has_final=final_g is not None),
        out_shape=jax.ShapeDtypeStruct((n, D_MODEL), F32),
        grid=(n // tm,),
        in_specs=specs,
        out_specs=row,
        scratch_shapes=scratch,
        compiler_params=_params("arbitrary"),
        name="ffn",
    )(*args)


def _proj_kernel(h_ref, g_ref, w_ref, b_ref, c_ref, s1_ref, s2_ref, o_ref, xn_ref, *,
                 n_out, rope_cols, scale_cols, scale, chunk, head_major):
    xn_ref[...] = _rms_norm(h_ref[...], g_ref[...]).astype(BF16)
    c, s1, s2 = c_ref[...], s1_ref[...], s2_ref[...]
    for c0 in range(0, n_out, chunk):
        cols = slice(c0, c0 + chunk)
        y = jnp.dot(xn_ref[...], w_ref[:, cols], preferred_element_type=F32) + b_ref[:, cols]
        for j in range(0, chunk, LANES):
            yj = y[:, j:j + LANES]
            if c0 + j < rope_cols:
                yj = _apply_rope(yj, c, s1, s2)
            if c0 + j < scale_cols:
                yj = yj * scale
            if head_major:
                o_ref[(c0 + j) // LANES] = yj.astype(BF16)
            else:
                o_ref[:, c0 + j:c0 + j + LANES] = yj.astype(BF16)


def _proj(h, g, w, b, tables, *, rope_cols, scale_cols, scale, head_major=False, tm=512):
    n = h.shape[0]
    n_out = w.shape[1]
    chunk = min(n_out, 512)
    assert rope_cols % LANES == 0 and scale_cols % LANES == 0 and n_out % chunk == 0
    row = pl.BlockSpec((tm, D_MODEL), lambda i: (i, 0))
    tab = pl.BlockSpec((tm, LANES), lambda i: (i, 0))
    if head_major:
        out_shape = jax.ShapeDtypeStruct((n_out // LANES, n, LANES), BF16)
        out_spec = pl.BlockSpec((n_out // LANES, tm, LANES), lambda i: (0, i, 0))
    else:
        out_shape = jax.ShapeDtypeStruct((n, n_out), BF16)
        out_spec = pl.BlockSpec((tm, n_out), lambda i: (i, 0))
    return pl.pallas_call(
        functools.partial(_proj_kernel, n_out=n_out, rope_cols=rope_cols, scale_cols=scale_cols,
                          scale=scale, chunk=chunk, head_major=head_major),
        out_shape=out_shape,
        grid=(n // tm,),
        in_specs=[row, _resident((1, D_MODEL)), _resident((D_MODEL, n_out)), _resident((1, n_out)),
                  tab, tab, tab],
        out_specs=out_spec,
        scratch_shapes=[pltpu.VMEM((tm, D_MODEL), BF16)],
        compiler_params=_params("arbitrary"),
        name="norm_proj",
    )(h, g.reshape(1, D_MODEL), w, b.reshape(1, n_out), *tables)


def _diff_attn_kernel(lam_ref, gsub_ref, q_ref, k_ref, v_ref, o_ref, s_ref, m_ref, acc_ref, *,
                      t, hp, lambda_init):
    i = pl.program_id(2)
    lane = lax.broadcasted_iota(jnp.int32, (t, LANES), 1)
    ones = jnp.ones((t, LANES), BF16)
    q2 = []
    for hh in range(hp):
        q = q_ref[hh, 0]
        zero = jnp.zeros_like(q)
        q2.append(jnp.concatenate([jnp.where(lane < DIFF_SUB_DIM, q, zero),
                                   jnp.where(lane >= DIFF_SUB_DIM, q, zero)], axis=0))

    def scores(hh, j):
        k = k_ref[hh, 0, pl.ds(j * t, t), :]
        return lax.dot_general(q2[hh], k, (((1,), (1,)), ((), ())), preferred_element_type=F32)

    def attend(hh, j, masked):
        s = s_ref[hh]
        if masked:
            row = lax.broadcasted_iota(jnp.int32, s.shape, 0) % t
            col = lax.broadcasted_iota(jnp.int32, s.shape, 1)
            s = jnp.where(col <= row, s, MASK_VALUE)
        m_prev = m_ref[hh]
        m_next = jnp.maximum(m_prev, jnp.max(s, axis=-1, keepdims=True))
        m_ref[hh] = m_next
        alpha = jnp.tile(jnp.exp2(m_prev - m_next), (1, 2))
        p = jnp.exp2(s - jnp.tile(m_next, (1, t // LANES))).astype(BF16)
        v_aug = jnp.concatenate([v_ref[hh, 0, pl.ds(j * t, t), :], ones], axis=1)
        return alpha * acc_ref[hh] + jnp.dot(p, v_aug, preferred_element_type=F32)

    m_ref[...] = jnp.full(m_ref.shape, MASK_VALUE, F32)
    acc_ref[...] = jnp.zeros(acc_ref.shape, F32)
    for hh in range(hp):
        s_ref[hh] = scores(hh, 0)

    def body(j, carry):
        for hh in range(hp):
            acc = attend(hh, j, False)
            s_ref[hh] = scores(hh, j + 1)
            acc_ref[hh] = acc
        return carry

    lax.fori_loop(0, i, body, 0)

    lv = lam_ref[...]
    lam = (jnp.exp(jnp.sum(lv[0:1] * lv[1:2], axis=-1, keepdims=True))
           - jnp.exp(jnp.sum(lv[2:3] * lv[3:4], axis=-1, keepdims=True)) + lambda_init)
    for hh in range(hp):
        acc = attend(hh, i, True)
        attn = acc[:, :DIFF_V_DIM] / acc[:, DIFF_V_DIM:]
        o = attn[:t] - lam * attn[t:]
        o = _rms_norm(o, gsub_ref[...]) * (1.0 - lambda_init)
        o_ref[0, :, hh * DIFF_V_DIM:(hh + 1) * DIFF_V_DIM] = o.astype(BF16)


def _diff_attention(qkv, lam_vecs, g_sub, lambda_init, *, t=512, hp=2):
    _, b, s, _ = qkv.shape
    hg = DIFF_HEADS // hp
    return pl.pallas_call(
        functools.partial(_diff_attn_kernel, t=t, hp=hp, lambda_init=lambda_init),
        out_shape=jax.ShapeDtypeStruct((b, s, D_MODEL), BF16),
        grid=(b, hg, s // t),
        in_specs=[
            pl.BlockSpec((4, DIFF_SUB_DIM), lambda bi, hi, i: (0, 0)),
            pl.BlockSpec((1, DIFF_V_DIM), lambda bi, hi, i: (0, 0)),
            pl.BlockSpec((hp, 1, t, DIFF_V_DIM), lambda bi, hi, i: (hi, bi, i, 0)),
            pl.BlockSpec((hp, 1, s, DIFF_V_DIM), lambda bi, hi, i: (hg + hi, bi, 0, 0)),
            pl.BlockSpec((hp, 1, s, DIFF_V_DIM), lambda bi, hi, i: (2 * hg + hi, bi, 0, 0)),
        ],
        out_specs=pl.BlockSpec((1, t, hp * DIFF_V_DIM), lambda bi, hi, i: (bi, i, hi)),
        scratch_shapes=[pltpu.VMEM((hp, 2 * t, t), F32), pltpu.VMEM((hp, 2 * t, LANES), F32),
                        pltpu.VMEM((hp, 2 * t, 2 * DIFF_V_DIM), F32)],
        compiler_params=_params("arbitrary", "arbitrary", "arbitrary"),
        name="diff_attention",
    )(lam_vecs, g_sub.reshape(1, DIFF_V_DIM), qkv, qkv, qkv)


def _swa_kernel(sink_ref, q_ref, kp_ref, kc_ref, vp_ref, vc_ref, o_ref, bias_ref, *, tq, pair):
    i = pl.program_id(1)
    kk = jnp.concatenate([kp_ref[0], kc_ref[0]], axis=0)
    vv = jnp.concatenate([vp_ref[0], vc_ref[0]], axis=0)
    ones = jnp.ones((2 * BLOCK, LANES), BF16)
    rows = 2 * pair * BLOCK
    lane_q = lax.broadcasted_iota(jnp.int32, (BLOCK, LANES), 1)
    qi = lax.broadcasted_iota(jnp.int32, (BLOCK, 2 * BLOCK), 0)
    kj = lax.broadcasted_iota(jnp.int32, (BLOCK, 2 * BLOCK), 1)
    in_window = (kj > qi) & (kj <= qi + WINDOW)
    bias_ref[0] = jnp.where(in_window, 0.0, MASK_VALUE)
    bias_ref[1] = jnp.where(in_window & (kj >= jnp.where(i > 0, 0, BLOCK)), 0.0, MASK_VALUE)
    for c in range(tq // BLOCK):
        blk = slice(c * BLOCK, (c + 1) * BLOCK)
        k = kk[c * BLOCK:(c + 2) * BLOCK]
        v_aug = jnp.concatenate([vv[c * BLOCK:(c + 2) * BLOCK], ones], axis=1)
        for jc in range(D_MODEL // LANES // pair):
            parts = []
            for j in range(jc * pair, (jc + 1) * pair):
                qj = q_ref[0, blk, j * LANES:(j + 1) * LANES]
                parts.append(jnp.where(lane_q < SWA_HEAD_DIM, qj, jnp.zeros_like(qj)))
                parts.append(jnp.where(lane_q >= SWA_HEAD_DIM, qj, jnp.zeros_like(qj)))
            q2 = jnp.concatenate(parts, axis=0)
            s = lax.dot_general(q2, k, (((1,), (1,)), ((), ())), preferred_element_type=F32)
            s = s + jnp.tile(bias_ref[1 if c == 0 else 0], (rows // BLOCK, 1))
            sink = sink_ref[jc * rows:(jc + 1) * rows, :]
            m = jnp.maximum(jnp.max(s, axis=-1, keepdims=True), sink)
            p = jnp.exp2(s - jnp.tile(m, (1, 2))).astype(BF16)
            pv = jnp.dot(p, v_aug, preferred_element_type=F32)
            o2 = pv[:, :LANES] / (pv[:, LANES:] + jnp.exp2(sink - m))
            for jj in range(pair):
                j = jc * pair + jj
                lo = o2[(2 * jj) * BLOCK:(2 * jj + 1) * BLOCK]
                hi = o2[(2 * jj + 1) * BLOCK:(2 * jj + 2) * BLOCK]
                o_ref[0, blk, j * LANES:(j + 1) * LANES] = (
                    jnp.where(lane_q < SWA_HEAD_DIM, lo, hi).astype(BF16))


def _swa_attention(q, kv, sink_rows, *, tq=512, pair=2):
    b, s, _ = q.shape
    per = tq // BLOCK
    cur = lambda bi, i: (bi, i, 0)
    prev = lambda bi, i: (bi, jnp.maximum(i * per - 1, 0), 0)
    cur_v = lambda bi, i: (bi, i, 1)
    prev_v = lambda bi, i: (bi, jnp.maximum(i * per - 1, 0), 1)
    rows = sink_rows.shape[0]
    return pl.pallas_call(
        functools.partial(_swa_kernel, tq=tq, pair=pair),
        out_shape=jax.ShapeDtypeStruct((b, s, D_MODEL), BF16),
        grid=(b, s // tq),
        in_specs=[
            pl.BlockSpec((rows, LANES), lambda bi, i: (0, 0)),
            pl.BlockSpec((1, tq, D_MODEL), cur),
            pl.BlockSpec((1, BLOCK, LANES), prev),
            pl.BlockSpec((1, tq, LANES), cur),
            pl.BlockSpec((1, BLOCK, LANES), prev_v),
            pl.BlockSpec((1, tq, LANES), cur_v),
        ],
        out_specs=pl.BlockSpec((1, tq, D_MODEL), cur),
        scratch_shapes=[pltpu.VMEM((2, BLOCK, 2 * BLOCK), F32)],
        compiler_params=_params("arbitrary", "arbitrary"),
        name="swa_attention",
    )(sink_rows, q, kv, kv, kv, kv)


def _paired_head_order():
    cols = []
    for j in range(SWA_GROUP):
        cols.extend(range(j * SWA_HEAD_DIM, (j + 1) * SWA_HEAD_DIM))
        cols.extend(range((SWA_GROUP + j) * SWA_HEAD_DIM, (SWA_GROUP + j + 1) * SWA_HEAD_DIM))
    return jnp.asarray(cols, dtype=jnp.int32)


def kernel(x, positions, ln_ffn1, ffn1_w_gate, ffn1_w_up, ffn1_w_down, ln_mix, ln_ffn2, ffn2_w_gate, ffn2_w_up, ffn2_w_down, a_w_qkv, a_w_o, a_lambda_q1, a_lambda_k1, a_lambda_q2, a_lambda_k2, a_subln, b_w_q, b_b_q, b_sinks, b_w_o, b_b_o, kv_norm, kv_w_k, kv_b_k, kv_w_v, kv_b_v, final_norm):
    batch, seq, _ = x.shape
    n = batch * seq
    tables = _rope_tables(positions)
    h = x.reshape(n, D_MODEL)
    perm = _paired_head_order()
    kv = None
    for layer in range(DEPTH):
        h = _ffn(h, ln_ffn1[layer], ffn1_w_gate[layer].astype(BF16), ffn1_w_up[layer].astype(BF16),
                 ffn1_w_down[layer].astype(BF16))
        if layer < N_A_LAYERS:
            a = layer
            lambda_init = 0.8 - 0.6 * math.exp(-0.3 * layer)
            qkv = _proj(h, ln_mix[layer], a_w_qkv[a].astype(BF16), jnp.zeros((3 * D_MODEL,), F32),
                        tables, rope_cols=2 * D_MODEL, scale_cols=D_MODEL,
                        scale=DIFF_SUB_DIM ** -0.5 * LOG2_E, head_major=True)
            lam_vecs = jnp.stack([a_lambda_q1[a], a_lambda_k1[a], a_lambda_q2[a], a_lambda_k2[a]])
            o = _diff_attention(qkv.reshape(3 * DIFF_HEADS, batch, seq, DIFF_V_DIM), lam_vecs,
                                a_subln[a], lambda_init)
            attn = (o.reshape(n, D_MODEL), a_w_o[a].astype(BF16), jnp.zeros((D_MODEL,), F32))
        else:
            b = layer - N_A_LAYERS
            q = _proj(h, ln_mix[layer], b_w_q[b][:, perm].astype(BF16), b_b_q[b][perm], tables,
                      rope_cols=D_MODEL, scale_cols=D_MODEL, scale=SWA_HEAD_DIM ** -0.5 * LOG2_E)
            sink_pairs = jnp.stack([b_sinks[b][:SWA_GROUP], b_sinks[b][SWA_GROUP:]], axis=1)
            sink_rows = jnp.broadcast_to(
                jnp.repeat(sink_pairs.reshape(-1) * LOG2_E, BLOCK)[:, None], (2 * D_MODEL, LANES))
            o = _swa_attention(q.reshape(batch, seq, D_MODEL), kv, sink_rows)
            attn = (o.reshape(n, D_MODEL), b_w_o[b][perm, :].astype(BF16), b_b_o[b])
        h = _ffn(h, ln_ffn2[layer], ffn2_w_gate[layer].astype(BF16), ffn2_w_up[layer].astype(BF16),
                 ffn2_w_down[layer].astype(BF16), attn=attn,
                 final_g=final_norm if layer == DEPTH - 1 else None)
        if layer == N_A_LAYERS - 1:
            w_kv = jnp.concatenate([kv_w_k, kv_w_v], axis=1).astype(BF16)
            b_kv = jnp.concatenate([kv_b_k, kv_b_v])
            kv = _proj(h, kv_norm, w_kv, b_kv, tables, rope_cols=SWA_KV_WIDTH, scale_cols=0,
                       scale=1.0).reshape(batch, seq, 2 * SWA_KV_WIDTH)
    return h.reshape(batch, seq, D_MODEL)
```

```python
import functools
import math

import jax
import jax.numpy as jnp
from jax import lax
from jax.experimental import pallas as pl
from jax.experimental.pallas import tpu as pltpu

D_MODEL = 1024
D_FF = 2816
DEPTH = 4
N_A_LAYERS = 2
DIFF_HEADS = 8
DIFF_SUB_DIM = 64
DIFF_V_DIM = 128
SWA_Q_HEADS = 16
SWA_KV_HEADS = 2
SWA_HEAD_DIM = 64
SWA_GROUP = SWA_Q_HEADS // SWA_KV_HEADS
SWA_KV_WIDTH = SWA_KV_HEADS * SWA_HEAD_DIM
WINDOW = 128
BLOCK = 128
ROPE_THETA = 500000.0
ROT_DIM = 16
ROT_HALF = ROT_DIM // 2
NORM_EPS = 1e-5

LANES = 128
VMEM_LIMIT = 56 * 1024 * 1024
MASK_VALUE = -1e30
LOG2_E = math.log2(math.e)

F32 = jnp.float32
BF16 = jnp.bfloat16


def _params(*sem):
    return pltpu.CompilerParams(dimension_semantics=sem, vmem_limit_bytes=VMEM_LIMIT)


def _resident(shape):
    return pl.BlockSpec(shape, lambda *_: (0,) * len(shape), pipeline_mode=pl.Buffered(1))


def _rms_norm(x, g):
    ms = jnp.mean(x * x, axis=-1, keepdims=True)
    return x * lax.rsqrt(ms + NORM_EPS) * g


def _rope_table_kernel(pos_ref, invf_ref, c_ref, s1_ref, s2_ref):
    ang = pos_ref[...] * invf_ref[...]
    cos = jnp.cos(ang)
    sin = jnp.sin(ang)
    lane = lax.broadcasted_iota(jnp.int32, ang.shape, 1) % SWA_HEAD_DIM
    first = lane < ROT_HALF
    second = (lane >= ROT_HALF) & (lane < ROT_DIM)
    c_ref[...] = jnp.where(first | second, cos, 1.0)
    s1_ref[...] = jnp.where(first, -sin, 0.0)
    s2_ref[...] = jnp.where(second, sin, 0.0)


def _rope_tables(positions, tm=1024):
    n = positions.size
    pos = jnp.broadcast_to(positions.reshape(n, 1).astype(F32), (n, LANES))
    inv_freq = ROPE_THETA ** (-jnp.arange(0, ROT_DIM, 2, dtype=F32) / ROT_DIM)
    invf = jnp.tile(inv_freq, LANES // ROT_HALF).reshape(1, LANES)
    spec = pl.BlockSpec((tm, LANES), lambda i: (i, 0))
    out = jax.ShapeDtypeStruct((n, LANES), F32)
    return pl.pallas_call(
        _rope_table_kernel,
        out_shape=(out, out, out),
        grid=(n // tm,),
        in_specs=[spec, pl.BlockSpec((1, LANES), lambda i: (0, 0))],
        out_specs=(spec, spec, spec),
        compiler_params=_params("arbitrary"),
        name="rope_tables",
    )(pos, invf)


def _apply_rope(x, c, s1, s2):
    return (x * c + pltpu.roll(x, LANES - ROT_HALF, 1) * s1 + pltpu.roll(x, ROT_HALF, 1) * s2)


def _ffn_kernel(*refs, tf, has_attn, has_final):
    h_ref, g_ref, wg_ref, wu_ref, wd_ref = refs[:5]
    rest = list(refs[5:])
    if has_attn:
        a_ref, wo_ref, bo_ref = rest[:3]
        rest = rest[3:]
    if has_final:
        gf_ref = rest.pop(0)
    o_ref, xn_ref, acc_ref = rest[:3]
    h = h_ref[...]
    if has_attn:
        res_ref = rest[3]
        h = h + bo_ref[...] + jnp.dot(a_ref[...], wo_ref[...], preferred_element_type=F32)
        res_ref[...] = h
    else:
        res_ref = h_ref
    xn_ref[...] = _rms_norm(h, g_ref[...]).astype(BF16)
    for c in range(D_FF // tf):
        cols = slice(c * tf, (c + 1) * tf)
        xn = xn_ref[...]
        gate = jnp.dot(xn, wg_ref[:, cols], preferred_element_type=F32)
        up = jnp.dot(xn, wu_ref[:, cols], preferred_element_type=F32)
        act = (gate / (1.0 + jnp.exp(-gate)) * up).astype(BF16)
        down = jnp.dot(act, wd_ref[cols, :], preferred_element_type=F32)
        if c == 0:
            acc_ref[...] = down
        else:
            acc_ref[...] += down
    out = res_ref[...] + 0.5 * acc_ref[...]
    if has_final:
        out = _rms_norm(out, gf_ref[...])
    o_ref[...] = out


def _ffn(h, g, wg, wu, wd, *, attn=None, final_g=None, tm=512, tf=256):
    n = h.shape[0]
    row = pl.BlockSpec((tm, D_MODEL), lambda i: (i, 0))
    vec = _resident((1, D_MODEL))
    args = [h, g.reshape(1, D_MODEL), wg, wu, wd]
    specs = [row, vec, _resident((D_MODEL, D_FF)), _resident((D_MODEL, D_FF)),
             _resident((D_FF, D_MODEL))]
    scratch = [pltpu.VMEM((tm, D_MODEL), BF16), pltpu.VMEM((tm, D_MODEL), F32)]
    if attn is not None:
        a, wo, bo = attn
        args += [a, wo, bo.reshape(1, D_MODEL)]
        specs += [row, _resident((D_MODEL, D_MODEL)), vec]
        scratch.append(pltpu.VMEM((tm, D_MODEL), F32))
    if final_g is not None:
        args.append(final_g.reshape(1, D_MODEL))
        specs.append(vec)
    return pl.pallas_call(
        functools.partial(_ffn_kernel, tf=tf, has_attn=attn is not None,
                          has_final=final_g is not None),
        out_shape=jax.ShapeDtypeStruct((n, D_MODEL), F32),
        grid=(n // tm,),
        in_specs=specs,
        out_specs=row,
        scratch_shapes=scratch,
        compiler_params=_params("arbitrary"),
        name="ffn",
    )(*args)


def _proj_kernel(h_ref, g_ref, w_ref, b_ref, c_ref, s1_ref, s2_ref, o_ref, xn_ref, *,
                 n_out, rope_cols, scale_cols, scale, chunk, head_major):
    xn_ref[...] = _rms_norm(h_ref[...], g_ref[...]).astype(BF16)
    c, s1, s2 = c_ref[...], s1_ref[...], s2_ref[...]
    for c0 in range(0, n_out, chunk):
        cols = slice(c0, c0 + chunk)
        y = jnp.dot(xn_ref[...], w_ref[:, cols], preferred_element_type=F32) + b_ref[:, cols]
        for j in range(0, chunk, LANES):
            yj = y[:, j:j + LANES]
            if c0 + j < rope_cols:
                yj = _apply_rope(yj, c, s1, s2)
            if c0 + j < scale_cols:
                yj = yj * scale
            if head_major:
                o_ref[(c0 + j) // LANES] = yj.astype(BF16)
            else:
                o_ref[:, c0 + j:c0 + j + LANES] = yj.astype(BF16)


def _proj(h, g, w, b, tables, *, rope_cols, scale_cols, scale, head_major=False, tm=512):
    n = h.shape[0]
    n_out = w.shape[1]
    chunk = min(n_out, 512)
    assert rope_cols % LANES == 0 and scale_cols % LANES == 0 and n_out % chunk == 0
    row = pl.BlockSpec((tm, D_MODEL), lambda i: (i, 0))
    tab = pl.BlockSpec((tm, LANES), lambda i: (i, 0))
    if head_major:
        out_shape = jax.ShapeDtypeStruct((n_out // LANES, n, LANES), BF16)
        out_spec = pl.BlockSpec((n_out // LANES, tm, LANES), lambda i: (0, i, 0))
    else:
        out_shape = jax.ShapeDtypeStruct((n, n_out), BF16)
        out_spec = pl.BlockSpec((tm, n_out), lambda i: (i, 0))
    return pl.pallas_call(
        functools.partial(_proj_kernel, n_out=n_out, rope_cols=rope_cols, scale_cols=scale_cols,
                          scale=scale, chunk=chunk, head_major=head_major),
        out_shape=out_shape,
        grid=(n // tm,),
        in_specs=[row, _resident((1, D_MODEL)), _resident((D_MODEL, n_out)), _resident((1, n_out)),
                  tab, tab, tab],
        out_specs=out_spec,
        scratch_shapes=[pltpu.VMEM((tm, D_MODEL), BF16)],
        compiler_params=_params("arbitrary"),
        name="norm_proj",
    )(h, g.reshape(1, D_MODEL), w, b.reshape(1, n_out), *tables)


def _diff_attn_kernel(lam_ref, gsub_ref, q_ref, k_ref, v_ref, o_ref, s_ref, m_ref, acc_ref, *,
                      t, hp, lambda_init):
    i = pl.program_id(2)
    lane = lax.broadcasted_iota(jnp.int32, (t, LANES), 1)
    ones = jnp.ones((t, LANES), BF16)
    q2 = []
    for hh in range(hp):
        q = q_ref[hh, 0]
        zero = jnp.zeros_like(q)
        q2.append(jnp.concatenate([jnp.where(lane < DIFF_SUB_DIM, q, zero),
                                   jnp.where(lane >= DIFF_SUB_DIM, q, zero)], axis=0))

    def scores(hh, j):
        k = k_ref[hh, 0, pl.ds(j * t, t), :]
        return lax.dot_general(q2[hh], k, (((1,), (1,)), ((), ())), preferred_element_type=F32)

    def attend(hh, j, masked):
        s = s_ref[hh]
        if masked:
            row = lax.broadcasted_iota(jnp.int32, s.shape, 0) % t
            col = lax.broadcasted_iota(jnp.int32, s.shape, 1)
            s = jnp.where(col <= row, s, MASK_VALUE)
        m_prev = m_ref[hh]
        m_next = jnp.maximum(m_prev, jnp.max(s, axis=-1, keepdims=True))
        m_ref[hh] = m_next
        alpha = jnp.tile(jnp.exp2(m_prev - m_next), (1, 2))
        p = jnp.exp2(s - jnp.tile(m_next, (1, t // LANES))).astype(BF16)
        v_aug = jnp.concatenate([v_ref[hh, 0, pl.ds(j * t, t), :], ones], axis=1)
        return alpha * acc_ref[hh] + jnp.dot(p, v_aug, preferred_element_type=F32)

    m_ref[...] = jnp.full(m_ref.shape, MASK_VALUE, F32)
    acc_ref[...] = jnp.zeros(acc_ref.shape, F32)
    for hh in range(hp):
        s_ref[hh] = scores(hh, 0)

    def body(j, carry):
        for hh in range(hp):
            acc = attend(hh, j, False)
            s_ref[hh] = scores(hh, j + 1)
            acc_ref[hh] = acc
        return carry

    lax.fori_loop(0, i, body, 0)

    lv = lam_ref[...]
    lam = (jnp.exp(jnp.sum(lv[0:1] * lv[1:2], axis=-1, keepdims=True))
           - jnp.exp(jnp.sum(lv[2:3] * lv[3:4], axis=-1, keepdims=True)) + lambda_init)
    for hh in range(hp):
        acc = attend(hh, i, True)
        attn = acc[:, :DIFF_V_DIM] / acc[:, DIFF_V_DIM:]
        o = attn[:t] - lam * attn[t:]
        o = _rms_norm(o, gsub_ref[...]) * (1.0 - lambda_init)
        o_ref[0, :, hh * DIFF_V_DIM:(hh + 1) * DIFF_V_DIM] = o.astype(BF16)


def _diff_attention(qkv, lam_vecs, g_sub, lambda_init, *, t=512, hp=4):
    _, b, s, _ = qkv.shape
    hg = DIFF_HEADS // hp
    return pl.pallas_call(
        functools.partial(_diff_attn_kernel, t=t, hp=hp, lambda_init=lambda_init),
        out_shape=jax.ShapeDtypeStruct((b, s, D_MODEL), BF16),
        grid=(b, hg, s // t),
        in_specs=[
            pl.BlockSpec((4, DIFF_SUB_DIM), lambda bi, hi, i: (0, 0)),
            pl.BlockSpec((1, DIFF_V_DIM), lambda bi, hi, i: (0, 0)),
            pl.BlockSpec((hp, 1, t, DIFF_V_DIM), lambda bi, hi, i: (hi, bi, i, 0)),
            pl.BlockSpec((hp, 1, s, DIFF_V_DIM), lambda bi, hi, i: (hg + hi, bi, 0, 0),
                         pipeline_mode=pl.Buffered(1)),
            pl.BlockSpec((hp, 1, s, DIFF_V_DIM), lambda bi, hi, i: (2 * hg + hi, bi, 0, 0),
                         pipeline_mode=pl.Buffered(1)),
        ],
        out_specs=pl.BlockSpec((1, t, hp * DIFF_V_DIM), lambda bi, hi, i: (bi, i, hi)),
        scratch_shapes=[pltpu.VMEM((hp, 2 * t, t), F32), pltpu.VMEM((hp, 2 * t, LANES), F32),
                        pltpu.VMEM((hp, 2 * t, 2 * DIFF_V_DIM), F32)],
        compiler_params=_params("arbitrary", "arbitrary", "arbitrary"),
        name="diff_attention",
    )(lam_vecs, g_sub.reshape(1, DIFF_V_DIM), qkv, qkv, qkv)


def _swa_kernel(sink_ref, q_ref, kp_ref, kc_ref, vp_ref, vc_ref, o_ref, bias_ref, *, tq, pair):
    i = pl.program_id(1)
    kk = jnp.concatenate([kp_ref[0], kc_ref[0]], axis=0)
    vv = jnp.concatenate([vp_ref[0], vc_ref[0]], axis=0)
    ones = jnp.ones((2 * BLOCK, LANES), BF16)
    rows = 2 * pair * BLOCK
    lane_q = lax.broadcasted_iota(jnp.int32, (BLOCK, LANES), 1)
    qi = lax.broadcasted_iota(jnp.int32, (BLOCK, 2 * BLOCK), 0)
    kj = lax.broadcasted_iota(jnp.int32, (BLOCK, 2 * BLOCK), 1)
    in_window = (kj > qi) & (kj <= qi + WINDOW)
    bias_ref[0] = jnp.where(in_window, 0.0, MASK_VALUE)
    bias_ref[1] = jnp.where(in_window & (kj >= jnp.where(i > 0, 0, BLOCK)), 0.0, MASK_VALUE)
    for c in range(tq // BLOCK):
        blk = slice(c * BLOCK, (c + 1) * BLOCK)
        k = kk[c * BLOCK:(c + 2) * BLOCK]
        v_aug = jnp.concatenate([vv[c * BLOCK:(c + 2) * BLOCK], ones], axis=1)
        for jc in range(D_MODEL // LANES // pair):
            parts = []
            for j in range(jc * pair, (jc + 1) * pair):
                qj = q_ref[0, blk, j * LANES:(j + 1) * LANES]
                parts.append(jnp.where(lane_q < SWA_HEAD_DIM, qj, jnp.zeros_like(qj)))
                parts.append(jnp.where(lane_q >= SWA_HEAD_DIM, qj, jnp.zeros_like(qj)))
            q2 = jnp.concatenate(parts, axis=0)
            s = lax.dot_general(q2, k, (((1,), (1,)), ((), ())), preferred_element_type=F32)
            s = s + jnp.tile(bias_ref[1 if c == 0 else 0], (rows // BLOCK, 1))
            sink = sink_ref[jc * rows:(jc + 1) * rows, :]
            m = jnp.maximum(jnp.max(s, axis=-1, keepdims=True), sink)
            p = jnp.exp2(s - jnp.tile(m, (1, 2))).astype(BF16)
            pv = jnp.dot(p, v_aug, preferred_element_type=F32)
            o2 = pv[:, :LANES] / (pv[:, LANES:] + jnp.exp2(sink - m))
            for jj in range(pair):
                j = jc * pair + jj
                lo = o2[(2 * jj) * BLOCK:(2 * jj + 1) * BLOCK]
                hi = o2[(2 * jj + 1) * BLOCK:(2 * jj + 2) * BLOCK]
                o_ref[0, blk, j * LANES:(j + 1) * LANES] = (
                    jnp.where(lane_q < SWA_HEAD_DIM, lo, hi).astype(BF16))


def _swa_attention(q, kv, sink_rows, *, tq=512, pair=2):
    b, s, _ = q.shape
    per = tq // BLOCK
    cur = lambda bi, i: (bi, i, 0)
    prev = lambda bi, i: (bi, jnp.maximum(i * per - 1, 0), 0)
    cur_v = lambda bi, i: (bi, i, 1)
    prev_v = lambda bi, i: (bi, jnp.maximum(i * per - 1, 0), 1)
    rows = sink_rows.shape[0]
    return pl.pallas_call(
        functools.partial(_swa_kernel, tq=tq, pair=pair),
        out_shape=jax.ShapeDtypeStruct((b, s, D_MODEL), BF16),
        grid=(b, s // tq),
        in_specs=[
            pl.BlockSpec((rows, LANES), lambda bi, i: (0, 0)),
            pl.BlockSpec((1, tq, D_MODEL), cur),
            pl.BlockSpec((1, BLOCK, LANES), prev),
            pl.BlockSpec((1, tq, LANES), cur),
            pl.BlockSpec((1, BLOCK, LANES), prev_v),
            pl.BlockSpec((1, tq, LANES), cur_v),
        ],
        out_specs=pl.BlockSpec((1, tq, D_MODEL), cur),
        scratch_shapes=[pltpu.VMEM((2, BLOCK, 2 * BLOCK), F32)],
        compiler_params=_params("arbitrary", "arbitrary"),
        name="swa_attention",
    )(sink_rows, q, kv, kv, kv, kv)


def _paired_head_order():
    cols = []
    for j in range(SWA_GROUP):
        cols.extend(range(j * SWA_HEAD_DIM, (j + 1) * SWA_HEAD_DIM))
        cols.extend(range((SWA_GROUP + j) * SWA_HEAD_DIM, (SWA_GROUP + j + 1) * SWA_HEAD_DIM))
    return jnp.asarray(cols, dtype=jnp.int32)


def kernel(x, positions, ln_ffn1, ffn1_w_gate, ffn1_w_up, ffn1_w_down, ln_mix, ln_ffn2, ffn2_w_gate, ffn2_w_up, ffn2_w_down, a_w_qkv, a_w_o, a_lambda_q1, a_lambda_k1, a_lambda_q2, a_lambda_k2, a_subln, b_w_q, b_b_q, b_sinks, b_w_o, b_b_o, kv_norm, kv_w_k, kv_b_k, kv_w_v, kv_b_v, final_norm):
    batch, seq, _ = x.shape
    n = batch * seq
    tables = _rope_tables(positions)
    h = x.reshape(n, D_MODEL)
    perm = _paired_head_order()
    kv = None
    for layer in range(DEPTH):
        h = _ffn(h, ln_ffn1[layer], ffn1_w_gate[layer].astype(BF16), ffn1_w_up[layer].astype(BF16),
                 ffn1_w_down[layer].astype(BF16))
        if layer < N_A_LAYERS:
            a = layer
            lambda_init = 0.8 - 0.6 * math.exp(-0.3 * layer)
            qkv = _proj(h, ln_mix[layer], a_w_qkv[a].astype(BF16), jnp.zeros((3 * D_MODEL,), F32),
                        tables, rope_cols=2 * D_MODEL, scale_cols=D_MODEL,
                        scale=DIFF_SUB_DIM ** -0.5 * LOG2_E, head_major=True)
            lam_vecs = jnp.stack([a_lambda_q1[a], a_lambda_k1[a], a_lambda_q2[a], a_lambda_k2[a]])
            o = _diff_attention(qkv.reshape(3 * DIFF_HEADS, batch, seq, DIFF_V_DIM), lam_vecs,
                                a_subln[a], lambda_init)
            attn = (o.reshape(n, D_MODEL), a_w_o[a].astype(BF16), jnp.zeros((D_MODEL,), F32))
        else:
            b = layer - N_A_LAYERS
            q = _proj(h, ln_mix[layer], b_w_q[b][:, perm].astype(BF16), b_b_q[b][perm], tables,
                      rope_cols=D_MODEL, scale_cols=D_MODEL, scale=SWA_HEAD_DIM ** -0.5 * LOG2_E)
            sink_pairs = jnp.stack([b_sinks[b][:SWA_GROUP], b_sinks[b][SWA_GROUP:]], axis=1)
            sink_rows = jnp.broadcast_to(
                jnp.repeat(sink_pairs.reshape(-1) * LOG2_E, BLOCK)[:, None], (2 * D_MODEL, LANES))
            o = _swa_attention(q.reshape(batch, seq, D_MODEL), kv, sink_rows)
            attn = (o.reshape(n, D_MODEL), b_w_o[b][perm, :].astype(BF16), b_b_o[b])
        h = _ffn(h, ln_ffn2[layer], ffn2_w_gate[layer].astype(BF16), ffn2_w_up[layer].astype(BF16),
                 ffn2_w_down[layer].astype(BF16), attn=attn,
                 final_g=final_norm if layer == DEPTH - 1 else None)
        if layer == N_A_LAYERS - 1:
            w_kv = jnp.concatenate([kv_w_k, kv_w_v], axis=1).astype(BF16)
            b_kv = jnp.concatenate([kv_b_k, kv_b_v])
            kv = _proj(h, kv_norm, w_kv, b_kv, tables, rope_cols=SWA_KV_WIDTH, scale_cols=0,
                       scale=1.0).reshape(batch, seq, 2 * SWA_KV_WIDTH)
    return h.reshape(batch, seq, D_MODEL)
```

```python
import functools
import math

import jax
import jax.numpy as jnp
from jax import lax
from jax.experimental import pallas as pl
from jax.experimental.pallas import tpu as pltpu

D_MODEL = 1024
D_FF = 2816
DEPTH = 4
N_A_LAYERS = 2
DIFF_HEADS = 8
DIFF_SUB_DIM = 64
DIFF_V_DIM = 128
SWA_Q_HEADS = 16
SWA_KV_HEADS = 2
SWA_HEAD_DIM = 64
SWA_GROUP = SWA_Q_HEADS // SWA_KV_HEADS
SWA_KV_WIDTH = SWA_KV_HEADS * SWA_HEAD_DIM
WINDOW = 128
BLOCK = 128
ROPE_THETA = 500000.0
ROT_DIM = 16
ROT_HALF = ROT_DIM // 2
NORM_EPS = 1e-5

LANES = 128
VMEM_LIMIT = 56 * 1024 * 1024
MASK_VALUE = -1e30
LOG2_E = math.log2(math.e)

F32 = jnp.float32
BF16 = jnp.bfloat16


def _params(*sem):
    return pltpu.CompilerParams(dimension_semantics=sem, vmem_limit_bytes=VMEM_LIMIT)


def _resident(shape):
    return pl.BlockSpec(shape, lambda *_: (0,) * len(shape), pipeline_mode=pl.Buffered(1))


def _rms_norm(x, g):
    ms = jnp.mean(x * x, axis=-1, keepdims=True)
    return x * lax.rsqrt(ms + NORM_EPS) * g


def _rope_table_kernel(pos_ref, invf_ref, c_ref, s1_ref, s2_ref):
    ang = pos_ref[...] * invf_ref[...]
    cos = jnp.cos(ang)
    sin = jnp.sin(ang)
    lane = lax.broadcasted_iota(jnp.int32, ang.shape, 1) % SWA_HEAD_DIM
    first = lane < ROT_HALF
    second = (lane >= ROT_HALF) & (lane < ROT_DIM)
    c_ref[...] = jnp.where(first | second, cos, 1.0)
    s1_ref[...] = jnp.where(first, -sin, 0.0)
    s2_ref[...] = jnp.where(second, sin, 0.0)


def _rope_tables(positions, tm=1024):
    n = positions.size
    pos = jnp.broadcast_to(positions.reshape(n, 1).astype(F32), (n, LANES))
    inv_freq = ROPE_THETA ** (-jnp.arange(0, ROT_DIM, 2, dtype=F32) / ROT_DIM)
    invf = jnp.tile(inv_freq, LANES // ROT_HALF).reshape(1, LANES)
    spec = pl.BlockSpec((tm, LANES), lambda i: (i, 0))
    out = jax.ShapeDtypeStruct((n, LANES), F32)
    return pl.pallas_call(
        _rope_table_kernel,
        out_shape=(out, out, out),
        grid=(n // tm,),
        in_specs=[spec, pl.BlockSpec((1, LANES), lambda i: (0, 0))],
        out_specs=(spec, spec, spec),
        compiler_params=_params("arbitrary"),
        name="rope_tables",
    )(pos, invf)


def _apply_rope(x, c, s1, s2):
    return (x * c + pltpu.roll(x, LANES - ROT_HALF, 1) * s1 + pltpu.roll(x, ROT_HALF, 1) * s2)


def _project(xn_ref, w_ref, b_ref, tables, o_ref, *, rope_cols, scale_cols, scale, head_major):
    n_out = w_ref.shape[1]
    chunk = min(n_out, 512)
    c, s1, s2 = tables
    for c0 in range(0, n_out, chunk):
        cols = slice(c0, c0 + chunk)
        y = jnp.dot(xn_ref[...], w_ref[:, cols], preferred_element_type=F32) + b_ref[:, cols]
        for j in range(0, chunk, LANES):
            yj = y[:, j:j + LANES]
            if c0 + j < rope_cols:
                yj = _apply_rope(yj, c, s1, s2)
            if c0 + j < scale_cols:
                yj = yj * scale
            if head_major:
                o_ref[(c0 + j) // LANES] = yj.astype(BF16)
            else:
                o_ref[:, c0 + j:c0 + j + LANES] = yj.astype(BF16)


def _ffn_kernel(*refs, tf, has_attn, has_final, proj):
    h_ref, g_ref, wg_ref, wu_ref, wd_ref = refs[:5]
    rest = list(refs[5:])
    if has_attn:
        a_ref, wo_ref, bo_ref = rest[:3]
        rest = rest[3:]
    if has_final:
        gf_ref = rest.pop(0)
    if proj is not None:
        gp_ref, wp_ref, bp_ref, c_ref, s1_ref, s2_ref = rest[:6]
        rest = rest[6:]
        p_ref = rest.pop(1)
    o_ref, xn_ref, acc_ref = rest[:3]
    h = h_ref[...]
    if has_attn:
        res_ref = rest[3]
        h = h + bo_ref[...] + jnp.dot(a_ref[...], wo_ref[...], preferred_element_type=F32)
        res_ref[...] = h
    else:
        res_ref = h_ref
    xn_ref[...] = _rms_norm(h, g_ref[...]).astype(BF16)
    for c in range(D_FF // tf):
        cols = slice(c * tf, (c + 1) * tf)
        xn = xn_ref[...]
        gate = jnp.dot(xn, wg_ref[:, cols], preferred_element_type=F32)
        up = jnp.dot(xn, wu_ref[:, cols], preferred_element_type=F32)
        act = (gate / (1.0 + jnp.exp(-gate)) * up).astype(BF16)
        down = jnp.dot(act, wd_ref[cols, :], preferred_element_type=F32)
        if c == 0:
            acc_ref[...] = down
        else:
            acc_ref[...] += down
    out = res_ref[...] + 0.5 * acc_ref[...]
    if has_final:
        out = _rms_norm(out, gf_ref[...])
    o_ref[...] = out
    if proj is not None:
        xn_ref[...] = _rms_norm(out, gp_ref[...]).astype(BF16)
        _project(xn_ref, wp_ref, bp_ref, (c_ref[...], s1_ref[...], s2_ref[...]), p_ref, **proj)


def _ffn(h, g, wg, wu, wd, *, attn=None, final_g=None, proj=None, tm=512, tf=256):
    n = h.shape[0]
    row = pl.BlockSpec((tm, D_MODEL), lambda i: (i, 0))
    vec = _resident((1, D_MODEL))
    args = [h, g.reshape(1, D_MODEL), wg, wu, wd]
    specs = [row, vec, _resident((D_MODEL, D_FF)), _resident((D_MODEL, D_FF)),
             _resident((D_FF, D_MODEL))]
    scratch = [pltpu.VMEM((tm, D_MODEL), BF16), pltpu.VMEM((tm, D_MODEL), F32)]
    out_shape = jax.ShapeDtypeStruct((n, D_MODEL), F32)
    out_specs = row
    if attn is not None:
        a, wo, bo = attn
        args += [a, wo, bo.reshape(1, D_MODEL)]
        specs += [row, _resident((D_MODEL, D_MODEL)), vec]
        scratch.append(pltpu.VMEM((tm, D_MODEL), F32))
    if final_g is not None:
        args.append(final_g.reshape(1, D_MODEL))
        specs.append(vec)
    options = None
    if proj is not None:
        gp, wp, bp, tables, options = proj
        n_out = wp.shape[1]
        assert (options["rope_cols"] % LANES == 0 and options["scale_cols"] % LANES == 0
                and n_out % min(n_out, 512) == 0)
        tab = pl.BlockSpec((tm, LANES), lambda i: (i, 0))
        args += [gp.reshape(1, D_MODEL), wp, bp.reshape(1, n_out), *tables]
        specs += [vec, _resident((D_MODEL, n_out)), _resident((1, n_out)), tab, tab, tab]
        if options["head_major"]:
            p_shape = jax.ShapeDtypeStruct((n_out // LANES, n, LANES), BF16)
            p_spec = pl.BlockSpec((n_out // LANES, tm, LANES), lambda i: (0, i, 0))
        else:
            p_shape = jax.ShapeDtypeStruct((n, n_out), BF16)
            p_spec = pl.BlockSpec((tm, n_out), lambda i: (i, 0))
        out_shape = (out_shape, p_shape)
        out_specs = (row, p_spec)
    return pl.pallas_call(
        functools.partial(_ffn_kernel, tf=tf, has_attn=attn is not None,
                          has_final=final_g is not None, proj=options),
        out_shape=out_shape,
        grid=(n // tm,),
        in_specs=specs,
        out_specs=out_specs,
        scratch_shapes=scratch,
        compiler_params=_params("arbitrary"),
        name="ffn",
    )(*args)


def _diff_attn_kernel(lam_ref, gsub_ref, q_ref, k_ref, v_ref, o_ref, s_ref, m_ref, acc_ref, *,
                      t, hp, lambda_init):
    i = pl.program_id(2)
    lane = lax.broadcasted_iota(jnp.int32, (t, LANES), 1)
    ones = jnp.ones((t, LANES), BF16)
    q2 = []
    for hh in range(hp):
        q = q_ref[hh, 0]
        zero = jnp.zeros_like(q)
        q2.append(jnp.concatenate([jnp.where(lane < DIFF_SUB_DIM, q, zero),
                                   jnp.where(lane >= DIFF_SUB_DIM, q, zero)], axis=0))

    def scores(hh, j):
        k = k_ref[hh, 0, pl.ds(j * t, t), :]
        return lax.dot_general(q2[hh], k, (((1,), (1,)), ((), ())), preferred_element_type=F32)

    def attend(hh, j, masked):
        s = s_ref[hh]
        if masked:
            row = lax.broadcasted_iota(jnp.int32, s.shape, 0) % t
            col = lax.broadcasted_iota(jnp.int32, s.shape, 1)
            s = jnp.where(col <= row, s, MASK_VALUE)
        m_prev = m_ref[hh]
        m_next = jnp.maximum(m_prev, jnp.max(s, axis=-1, keepdims=True))
        m_ref[hh] = m_next
        alpha = jnp.tile(jnp.exp2(m_prev - m_next), (1, 2))
        p = jnp.exp2(s - jnp.tile(m_next, (1, t // LANES))).astype(BF16)
        v_aug = jnp.concatenate([v_ref[hh, 0, pl.ds(j * t, t), :], ones], axis=1)
        return alpha * acc_ref[hh] + jnp.dot(p, v_aug, preferred_element_type=F32)

    m_ref[...] = jnp.full(m_ref.shape, MASK_VALUE, F32)
    acc_ref[...] = jnp.zeros(acc_ref.shape, F32)
    for hh in range(hp):
        s_ref[hh] = scores(hh, 0)

    def body(j, carry):
        for hh in range(hp):
            acc = attend(hh, j, False)
            s_ref[hh] = scores(hh, j + 1)
            acc_ref[hh] = acc
        return carry

    lax.fori_loop(0, i, body, 0)

    lv = lam_ref[...]
    lam = (jnp.exp(jnp.sum(lv[0:1] * lv[1:2], axis=-1, keepdims=True))
           - jnp.exp(jnp.sum(lv[2:3] * lv[3:4], axis=-1, keepdims=True)) + lambda_init)
    for hh in range(hp):
        acc = attend(hh, i, True)
        attn = acc[:, :DIFF_V_DIM] / acc[:, DIFF_V_DIM:]
        o = attn[:t] - lam * attn[t:]
        o = _rms_norm(o, gsub_ref[...]) * (1.0 - lambda_init)
        o_ref[0, :, hh * DIFF_V_DIM:(hh + 1) * DIFF_V_DIM] = o.astype(BF16)


def _diff_attention(qkv, lam_vecs, g_sub, lambda_init, *, t=512, hp=4):
    _, b, s, _ = qkv.shape
    hg = DIFF_HEADS // hp
    return pl.pallas_call(
        functools.partial(_diff_attn_kernel, t=t, hp=hp, lambda_init=lambda_init),
        out_shape=jax.ShapeDtypeStruct((b, s, D_MODEL), BF16),
        grid=(b, hg, s // t),
        in_specs=[
            pl.BlockSpec((4, DIFF_SUB_DIM), lambda bi, hi, i: (0, 0)),
            pl.BlockSpec((1, DIFF_V_DIM), lambda bi, hi, i: (0, 0)),
            pl.BlockSpec((hp, 1, t, DIFF_V_DIM), lambda bi, hi, i: (hi, bi, i, 0)),
            pl.BlockSpec((hp, 1, s, DIFF_V_DIM), lambda bi, hi, i: (hg + hi, bi, 0, 0),
                         pipeline_mode=pl.Buffered(1)),
            pl.BlockSpec((hp, 1, s, DIFF_V_DIM), lambda bi, hi, i: (2 * hg + hi, bi, 0, 0),
                         pipeline_mode=pl.Buffered(1)),
        ],
        out_specs=pl.BlockSpec((1, t, hp * DIFF_V_DIM), lambda bi, hi, i: (bi, i, hi)),
        scratch_shapes=[pltpu.VMEM((hp, 2 * t, t), F32), pltpu.VMEM((hp, 2 * t, LANES), F32),
                        pltpu.VMEM((hp, 2 * t, 2 * DIFF_V_DIM), F32)],
        compiler_params=_params("arbitrary", "arbitrary", "arbitrary"),
        name="diff_attention",
    )(lam_vecs, g_sub.reshape(1, DIFF_V_DIM), qkv, qkv, qkv)


def _swa_kernel(sink_ref, q_ref, kp_ref, kc_ref, vp_ref, vc_ref, o_ref, bias_ref, *, tq, pair):
    i = pl.program_id(1)
    kk = jnp.concatenate([kp_ref[0], kc_ref[0]], axis=0)
    vv = jnp.concatenate([vp_ref[0], vc_ref[0]], axis=0)
    ones = jnp.ones((2 * BLOCK, LANES), BF16)
    rows = 2 * pair * BLOCK
    lane_q = lax.broadcasted_iota(jnp.int32, (BLOCK, LANES), 1)
    qi = lax.broadcasted_iota(jnp.int32, (BLOCK, 2 * BLOCK), 0)
    kj = lax.broadcasted_iota(jnp.int32, (BLOCK, 2 * BLOCK), 1)
    in_window = (kj > qi) & (kj <= qi + WINDOW)
    bias_ref[0] = jnp.where(in_window, 0.0, MASK_VALUE)
    bias_ref[1] = jnp.where(in_window & (kj >= jnp.where(i > 0, 0, BLOCK)), 0.0, MASK_VALUE)
    for c in range(tq // BLOCK):
        blk = slice(c * BLOCK, (c + 1) * BLOCK)
        k = kk[c * BLOCK:(c + 2) * BLOCK]
        v_aug = jnp.concatenate([vv[c * BLOCK:(c + 2) * BLOCK], ones], axis=1)
        for jc in range(D_MODEL // LANES // pair):
            parts = []
            for j in range(jc * pair, (jc + 1) * pair):
                qj = q_ref[0, blk, j * LANES:(j + 1) * LANES]
                parts.append(jnp.where(lane_q < SWA_HEAD_DIM, qj, jnp.zeros_like(qj)))
                parts.append(jnp.where(lane_q >= SWA_HEAD_DIM, qj, jnp.zeros_like(qj)))
            q2 = jnp.concatenate(parts, axis=0)
            s = lax.dot_general(q2, k, (((1,), (1,)), ((), ())), preferred_element_type=F32)
            s = s + jnp.tile(bias_ref[1 if c == 0 else 0], (rows // BLOCK, 1))
            sink = sink_ref[jc * rows:(jc + 1) * rows, :]
            m = jnp.maximum(jnp.max(s, axis=-1, keepdims=True), sink)
            p = jnp.exp2(s - jnp.tile(m, (1, 2))).astype(BF16)
            pv = jnp.dot(p, v_aug, preferred_element_type=F32)
            o2 = pv[:, :LANES] / (pv[:, LANES:] + jnp.exp2(sink - m))
            for jj in range(pair):
                j = jc * pair + jj
                lo = o2[(2 * jj) * BLOCK:(2 * jj + 1) * BLOCK]
                hi = o2[(2 * jj + 1) * BLOCK:(2 * jj + 2) * BLOCK]
                o_ref[0, blk, j * LANES:(j + 1) * LANES] = (
                    jnp.where(lane_q < SWA_HEAD_DIM, lo, hi).astype(BF16))


def _swa_attention(q, kv, sink_rows, *, tq=512, pair=2):
    b, s, _ = q.shape
    per = tq // BLOCK
    cur = lambda bi, i: (bi, i, 0)
    prev = lambda bi, i: (bi, jnp.maximum(i * per - 1, 0), 0)
    cur_v = lambda bi, i: (bi, i, 1)
    prev_v = lambda bi, i: (bi, jnp.maximum(i * per - 1, 0), 1)
    rows = sink_rows.shape[0]
    return pl.pallas_call(
        functools.partial(_swa_kernel, tq=tq, pair=pair),
        out_shape=jax.ShapeDtypeStruct((b, s, D_MODEL), BF16),
        grid=(b, s // tq),
        in_specs=[
            pl.BlockSpec((rows, LANES), lambda bi, i: (0, 0)),
            pl.BlockSpec((1, tq, D_MODEL), cur),
            pl.BlockSpec((1, BLOCK, LANES), prev),
            pl.BlockSpec((1, tq, LANES), cur),
            pl.BlockSpec((1, BLOCK, LANES), prev_v),
            pl.BlockSpec((1, tq, LANES), cur_v),
        ],
        out_specs=pl.BlockSpec((1, tq, D_MODEL), cur),
        scratch_shapes=[pltpu.VMEM((2, BLOCK, 2 * BLOCK), F32)],
        compiler_params=_params("arbitrary", "arbitrary"),
        name="swa_attention",
    )(sink_rows, q, kv, kv, kv, kv)


def _paired_head_order():
    cols = []
    for j in range(SWA_GROUP):
        cols.extend(range(j * SWA_HEAD_DIM, (j + 1) * SWA_HEAD_DIM))
        cols.extend(range((SWA_GROUP + j) * SWA_HEAD_DIM, (SWA_GROUP + j + 1) * SWA_HEAD_DIM))
    return jnp.asarray(cols, dtype=jnp.int32)


def kernel(x, positions, ln_ffn1, ffn1_w_gate, ffn1_w_up, ffn1_w_down, ln_mix, ln_ffn2, ffn2_w_gate, ffn2_w_up, ffn2_w_down, a_w_qkv, a_w_o, a_lambda_q1, a_lambda_k1, a_lambda_q2, a_lambda_k2, a_subln, b_w_q, b_b_q, b_sinks, b_w_o, b_b_o, kv_norm, kv_w_k, kv_b_k, kv_w_v, kv_b_v, final_norm):
    batch, seq, _ = x.shape
    n = batch * seq
    tables = _rope_tables(positions)
    h = x.reshape(n, D_MODEL)
    perm = _paired_head_order()
    kv = None
    for layer in range(DEPTH):
        if layer < N_A_LAYERS:
            a = layer
            proj = (ln_mix[layer], a_w_qkv[a].astype(BF16), jnp.zeros((3 * D_MODEL,), F32), tables,
                    dict(rope_cols=2 * D_MODEL, scale_cols=D_MODEL,
                         scale=DIFF_SUB_DIM ** -0.5 * LOG2_E, head_major=True))
        else:
            b = layer - N_A_LAYERS
            proj = (ln_mix[layer], b_w_q[b][:, perm].astype(BF16), b_b_q[b][perm], tables,
                    dict(rope_cols=D_MODEL, scale_cols=D_MODEL,
                         scale=SWA_HEAD_DIM ** -0.5 * LOG2_E, head_major=False))
        h, mixer_in = _ffn(h, ln_ffn1[layer], ffn1_w_gate[layer].astype(BF16),
                           ffn1_w_up[layer].astype(BF16), ffn1_w_down[layer].astype(BF16), proj=proj)
        if layer < N_A_LAYERS:
            lambda_init = 0.8 - 0.6 * math.exp(-0.3 * layer)
            lam_vecs = jnp.stack([a_lambda_q1[a], a_lambda_k1[a], a_lambda_q2[a], a_lambda_k2[a]])
            o = _diff_attention(mixer_in.reshape(3 * DIFF_HEADS, batch, seq, DIFF_V_DIM), lam_vecs,
                                a_subln[a], lambda_init)
            attn = (o.reshape(n, D_MODEL), a_w_o[a].astype(BF16), jnp.zeros((D_MODEL,), F32))
        else:
            sink_pairs = jnp.stack([b_sinks[b][:SWA_GROUP], b_sinks[b][SWA_GROUP:]], axis=1)
            sink_rows = jnp.broadcast_to(
                jnp.repeat(sink_pairs.reshape(-1) * LOG2_E, BLOCK)[:, None], (2 * D_MODEL, LANES))
            o = _swa_attention(mixer_in.reshape(batch, seq, D_MODEL), kv, sink_rows)
            attn = (o.reshape(n, D_MODEL), b_w_o[b][perm, :].astype(BF16), b_b_o[b])
        proj = None
        if layer == N_A_LAYERS - 1:
            proj = (kv_norm, jnp.concatenate([kv_w_k, kv_w_v], axis=1).astype(BF16),
                    jnp.concatenate([kv_b_k, kv_b_v]), tables,
                    dict(rope_cols=SWA_KV_WIDTH, scale_cols=0, scale=1.0, head_major=False))
        res = _ffn(h, ln_ffn2[layer], ffn2_w_gate[layer].astype(BF16), ffn2_w_up[layer].astype(BF16),
                   ffn2_w_down[layer].astype(BF16), attn=attn, proj=proj,
                   final_g=final_norm if layer == DEPTH - 1 else None)
        if proj is None:
            h = res
        else:
            h, kv = res[0], res[1].reshape(batch, seq, 2 * SWA_KV_WIDTH)
    return h.reshape(batch, seq, D_MODEL)
```

```python
import functools
import math

import jax
import jax.numpy as jnp
from jax import lax
from jax.experimental import pallas as pl
from jax.experimental.pallas import tpu as pltpu

D_MODEL = 1024
D_FF = 2816
DEPTH = 4
N_A_LAYERS = 2
DIFF_HEADS = 8
DIFF_SUB_DIM = 64
DIFF_V_DIM = 128
SWA_Q_HEADS = 16
SWA_KV_HEADS = 2
SWA_HEAD_DIM = 64
SWA_GROUP = SWA_Q_HEADS // SWA_KV_HEADS
SWA_KV_WIDTH = SWA_KV_HEADS * SWA_HEAD_DIM
WINDOW = 128
BLOCK = 128
ROPE_THETA = 500000.0
ROT_DIM = 16
ROT_HALF = ROT_DIM // 2
NORM_EPS = 1e-5

LANES = 128
VMEM_LIMIT = 56 * 1024 * 1024
MASK_VALUE = -1e30
LOG2_E = math.log2(math.e)

F32 = jnp.float32
BF16 = jnp.bfloat16


def _params(*sem):
    return pltpu.CompilerParams(dimension_semantics=sem, vmem_limit_bytes=VMEM_LIMIT)


def _resident(shape):
    return pl.BlockSpec(shape, lambda *_: (0,) * len(shape), pipeline_mode=pl.Buffered(1))


def _rms_norm(x, g):
    ms = jnp.mean(x * x, axis=-1, keepdims=True)
    return x * lax.rsqrt(ms + NORM_EPS) * g


def _rope_table_kernel(pos_ref, invf_ref, c_ref, s1_ref, s2_ref):
    ang = pos_ref[...] * invf_ref[...]
    cos = jnp.cos(ang)
    sin = jnp.sin(ang)
    lane = lax.broadcasted_iota(jnp.int32, ang.shape, 1) % SWA_HEAD_DIM
    first = lane < ROT_HALF
    second = (lane >= ROT_HALF) & (lane < ROT_DIM)
    c_ref[...] = jnp.where(first | second, cos, 1.0)
    s1_ref[...] = jnp.where(first, -sin, 0.0)
    s2_ref[...] = jnp.where(second, sin, 0.0)


def _rope_tables(positions, tm=1024):
    n = positions.size
    pos = jnp.broadcast_to(positions.reshape(n, 1).astype(F32), (n, LANES))
    inv_freq = ROPE_THETA ** (-jnp.arange(0, ROT_DIM, 2, dtype=F32) / ROT_DIM)
    invf = jnp.tile(inv_freq, LANES // ROT_HALF).reshape(1, LANES)
    spec = pl.BlockSpec((tm, LANES), lambda i: (i, 0))
    out = jax.ShapeDtypeStruct((n, LANES), F32)
    return pl.pallas_call(
        _rope_table_kernel,
        out_shape=(out, out, out),
        grid=(n // tm,),
        in_specs=[spec, pl.BlockSpec((1, LANES), lambda i: (0, 0))],
        out_specs=(spec, spec, spec),
        compiler_params=_params("arbitrary"),
        name="rope_tables",
    )(pos, invf)


def _apply_rope(x, c, s1, s2):
    return (x * c + pltpu.roll(x, LANES - ROT_HALF, 1) * s1 + pltpu.roll(x, ROT_HALF, 1) * s2)


def _project(xn_ref, w_ref, b_ref, tables, o_ref, *, rope_cols, scale_cols, scale, head_major):
    n_out = w_ref.shape[1]
    chunk = min(n_out, 512)
    c, s1, s2 = tables
    for c0 in range(0, n_out, chunk):
        cols = slice(c0, c0 + chunk)
        y = jnp.dot(xn_ref[...], w_ref[:, cols], preferred_element_type=F32) + b_ref[:, cols]
        for j in range(0, chunk, LANES):
            yj = y[:, j:j + LANES]
            if c0 + j < rope_cols:
                yj = _apply_rope(yj, c, s1, s2)
            if c0 + j < scale_cols:
                yj = yj * scale
            if head_major:
                o_ref[(c0 + j) // LANES] = yj.astype(BF16)
            else:
                o_ref[:, c0 + j:c0 + j + LANES] = yj.astype(BF16)


def _ffn_kernel(*refs, tf, has_attn, has_final, proj):
    h_ref, g_ref, wg_ref, wu_ref, wd_ref = refs[:5]
    rest = list(refs[5:])
    if has_attn:
        a_ref, wo_ref, bo_ref = rest[:3]
        rest = rest[3:]
    if has_final:
        gf_ref = rest.pop(0)
    if proj is not None:
        gp_ref, wp_ref, bp_ref, c_ref, s1_ref, s2_ref = rest[:6]
        rest = rest[6:]
        p_ref = rest.pop(1)
    o_ref, xn_ref, acc_ref = rest[:3]
    h = h_ref[...]
    if has_attn:
        res_ref = rest[3]
        h = h + bo_ref[...] + jnp.dot(a_ref[...], wo_ref[...], preferred_element_type=F32)
        res_ref[...] = h
    else:
        res_ref = h_ref
    xn_ref[...] = _rms_norm(h, g_ref[...]).astype(BF16)
    for c in range(D_FF // tf):
        cols = slice(c * tf, (c + 1) * tf)
        xn = xn_ref[...]
        gate = jnp.dot(xn, wg_ref[:, cols], preferred_element_type=F32)
        up = jnp.dot(xn, wu_ref[:, cols], preferred_element_type=F32)
        act = (gate / (1.0 + jnp.exp(-gate)) * up).astype(BF16)
        down = jnp.dot(act, wd_ref[cols, :], preferred_element_type=F32)
        if c == 0:
            acc_ref[...] = down
        else:
            acc_ref[...] += down
    out = res_ref[...] + 0.5 * acc_ref[...]
    if has_final:
        out = _rms_norm(out, gf_ref[...])
    o_ref[...] = out
    if proj is not None:
        xn_ref[...] = _rms_norm(out, gp_ref[...]).astype(BF16)
        _project(xn_ref, wp_ref, bp_ref, (c_ref[...], s1_ref[...], s2_ref[...]), p_ref, **proj)


def _ffn(h, g, wg, wu, wd, *, attn=None, final_g=None, proj=None, tm=512, tf=256):
    n = h.shape[0]
    row = pl.BlockSpec((tm, D_MODEL), lambda i: (i, 0))
    vec = _resident((1, D_MODEL))
    args = [h, g.reshape(1, D_MODEL), wg, wu, wd]
    specs = [row, vec, _resident((D_MODEL, D_FF)), _resident((D_MODEL, D_FF)),
             _resident((D_FF, D_MODEL))]
    scratch = [pltpu.VMEM((tm, D_MODEL), BF16), pltpu.VMEM((tm, D_MODEL), F32)]
    out_shape = jax.ShapeDtypeStruct((n, D_MODEL), F32)
    out_specs = row
    if attn is not None:
        a, wo, bo = attn
        args += [a, wo, bo.reshape(1, D_MODEL)]
        specs += [row, _resident((D_MODEL, D_MODEL)), vec]
        scratch.append(pltpu.VMEM((tm, D_MODEL), F32))
    if final_g is not None:
        args.append(final_g.reshape(1, D_MODEL))
        specs.append(vec)
    options = None
    if proj is not None:
        gp, wp, bp, tables, options = proj
        n_out = wp.shape[1]
        assert (options["rope_cols"] % LANES == 0 and options["scale_cols"] % LANES == 0
                and n_out % min(n_out, 512) == 0)
        tab = pl.BlockSpec((tm, LANES), lambda i: (i, 0))
        args += [gp.reshape(1, D_MODEL), wp, bp.reshape(1, n_out), *tables]
        specs += [vec, _resident((D_MODEL, n_out)), _resident((1, n_out)), tab, tab, tab]
        if options["head_major"]:
            p_shape = jax.ShapeDtypeStruct((n_out // LANES, n, LANES), BF16)
            p_spec = pl.BlockSpec((n_out // LANES, tm, LANES), lambda i: (0, i, 0))
        else:
            p_shape = jax.ShapeDtypeStruct((n, n_out), BF16)
            p_spec = pl.BlockSpec((tm, n_out), lambda i: (i, 0))
        out_shape = (out_shape, p_shape)
        out_specs = (row, p_spec)
    return pl.pallas_call(
        functools.partial(_ffn_kernel, tf=tf, has_attn=attn is not None,
                          has_final=final_g is not None, proj=options),
        out_shape=out_shape,
        grid=(n // tm,),
        in_specs=specs,
        out_specs=out_specs,
        scratch_shapes=scratch,
        compiler_params=_params("arbitrary"),
        name="ffn",
    )(*args)


def _diff_attn_kernel(lam_ref, gsub_ref, q_ref, k_ref, v_ref, o_ref, s_ref, m_ref, acc_ref, *,
                      t, hp, lambda_init):
    i = pl.program_id(2)
    lane = lax.broadcasted_iota(jnp.int32, (t, LANES), 1)
    ones = jnp.ones((t, LANES), BF16)
    q2 = []
    for hh in range(hp):
        q = q_ref[hh, 0]
        zero = jnp.zeros_like(q)
        q2.append(jnp.concatenate([jnp.where(lane < DIFF_SUB_DIM, q, zero),
                                   jnp.where(lane >= DIFF_SUB_DIM, q, zero)], axis=0))

    def scores(hh, j):
        k = k_ref[hh, 0, pl.ds(j * t, t), :]
        return lax.dot_general(q2[hh], k, (((1,), (1,)), ((), ())), preferred_element_type=F32)

    def attend(hh, j, masked):
        s = s_ref[hh]
        if masked:
            row = lax.broadcasted_iota(jnp.int32, s.shape, 0) % t
            col = lax.broadcasted_iota(jnp.int32, s.shape, 1)
            s = jnp.where(col <= row, s, MASK_VALUE)
        m_prev = m_ref[hh]
        m_next = jnp.maximum(m_prev, jnp.max(s, axis=-1, keepdims=True))
        m_ref[hh] = m_next
        alpha = jnp.tile(jnp.exp2(m_prev - m_next), (1, 2))
        p = jnp.exp2(s - jnp.tile(m_next, (1, t // LANES))).astype(BF16)
        v_aug = jnp.concatenate([v_ref[hh, 0, pl.ds(j * t, t), :], ones], axis=1)
        return alpha * acc_ref[hh] + jnp.dot(p, v_aug, preferred_element_type=F32)

    m_ref[...] = jnp.full(m_ref.shape, MASK_VALUE, F32)
    acc_ref[...] = jnp.zeros(acc_ref.shape, F32)
    for hh in range(hp):
        s_ref[hh] = scores(hh, 0)

    def body(j, carry):
        for hh in range(hp):
            acc = attend(hh, j, False)
            s_ref[hh] = scores(hh, j + 1)
            acc_ref[hh] = acc
        return carry

    lax.fori_loop(0, i, body, 0)

    lv = lam_ref[...]
    lam = (jnp.exp(jnp.sum(lv[0:1] * lv[1:2], axis=-1, keepdims=True))
           - jnp.exp(jnp.sum(lv[2:3] * lv[3:4], axis=-1, keepdims=True)) + lambda_init)
    for hh in range(hp):
        acc = attend(hh, i, True)
        attn = acc[:, :DIFF_V_DIM] / acc[:, DIFF_V_DIM:]
        o = attn[:t] - lam * attn[t:]
        o = _rms_norm(o, gsub_ref[...]) * (1.0 - lambda_init)
        o_ref[0, :, hh * DIFF_V_DIM:(hh + 1) * DIFF_V_DIM] = o.astype(BF16)


def _diff_attention(qkv, lam_vecs, g_sub, lambda_init, *, t=512, hp=4):
    _, b, s, _ = qkv.shape
    hg = DIFF_HEADS // hp
    return pl.pallas_call(
        functools.partial(_diff_attn_kernel, t=t, hp=hp, lambda_init=lambda_init),
        out_shape=jax.ShapeDtypeStruct((b, s, D_MODEL), BF16),
        grid=(b, hg, s // t),
        in_specs=[
            pl.BlockSpec((4, DIFF_SUB_DIM), lambda bi, hi, i: (0, 0)),
            pl.BlockSpec((1, DIFF_V_DIM), lambda bi, hi, i: (0, 0)),
            pl.BlockSpec((hp, 1, t, DIFF_V_DIM), lambda bi, hi, i: (hi, bi, i, 0)),
            pl.BlockSpec((hp, 1, s, DIFF_V_DIM), lambda bi, hi, i: (hg + hi, bi, 0, 0)),
            pl.BlockSpec((hp, 1, s, DIFF_V_DIM), lambda bi, hi, i: (2 * hg + hi, bi, 0, 0)),
        ],
        out_specs=pl.BlockSpec((1, t, hp * DIFF_V_DIM), lambda bi, hi, i: (bi, i, hi)),
        scratch_shapes=[pltpu.VMEM((hp, 2 * t, t), F32), pltpu.VMEM((hp, 2 * t, LANES), F32),
                        pltpu.VMEM((hp, 2 * t, 2 * DIFF_V_DIM), F32)],
        compiler_params=_params("arbitrary", "arbitrary", "arbitrary"),
        name="diff_attention",
    )(lam_vecs, g_sub.reshape(1, DIFF_V_DIM), qkv, qkv, qkv)


def _swa_kernel(sink_ref, q_ref, kp_ref, kc_ref, vp_ref, vc_ref, o_ref, bias_ref, *, tq, pair):
    i = pl.program_id(1)
    kk = jnp.concatenate([kp_ref[0], kc_ref[0]], axis=0)
    vv = jnp.concatenate([vp_ref[0], vc_ref[0]], axis=0)
    ones = jnp.ones((2 * BLOCK, LANES), BF16)
    rows = 2 * pair * BLOCK
    lane_q = lax.broadcasted_iota(jnp.int32, (BLOCK, LANES), 1)
    qi = lax.broadcasted_iota(jnp.int32, (BLOCK, 2 * BLOCK), 0)
    kj = lax.broadcasted_iota(jnp.int32, (BLOCK, 2 * BLOCK), 1)
    in_window = (kj > qi) & (kj <= qi + WINDOW)
    bias_ref[0] = jnp.where(in_window, 0.0, MASK_VALUE)
    bias_ref[1] = jnp.where(in_window & (kj >= jnp.where(i > 0, 0, BLOCK)), 0.0, MASK_VALUE)
    for c in range(tq // BLOCK):
        blk = slice(c * BLOCK, (c + 1) * BLOCK)
        k = kk[c * BLOCK:(c + 2) * BLOCK]
        v_aug = jnp.concatenate([vv[c * BLOCK:(c + 2) * BLOCK], ones], axis=1)
        for jc in range(D_MODEL // LANES // pair):
            parts = []
            for j in range(jc * pair, (jc + 1) * pair):
                qj = q_ref[0, blk, j * LANES:(j + 1) * LANES]
                parts.append(jnp.where(lane_q < SWA_HEAD_DIM, qj, jnp.zeros_like(qj)))
                parts.append(jnp.where(lane_q >= SWA_HEAD_DIM, qj, jnp.zeros_like(qj)))
            q2 = jnp.concatenate(parts, axis=0)
            s = lax.dot_general(q2, k, (((1,), (1,)), ((), ())), preferred_element_type=F32)
            s = s + jnp.tile(bias_ref[1 if c == 0 else 0], (rows // BLOCK, 1))
            sink = sink_ref[jc * rows:(jc + 1) * rows, :]
            m = jnp.maximum(jnp.max(s, axis=-1, keepdims=True), sink)
            p = jnp.exp2(s - jnp.tile(m, (1, 2))).astype(BF16)
            pv = jnp.dot(p, v_aug, preferred_element_type=F32)
            o2 = pv[:, :LANES] / (pv[:, LANES:] + jnp.exp2(sink - m))
            for jj in range(pair):
                j = jc * pair + jj
                lo = o2[(2 * jj) * BLOCK:(2 * jj + 1) * BLOCK]
                hi = o2[(2 * jj + 1) * BLOCK:(2 * jj + 2) * BLOCK]
                o_ref[0, blk, j * LANES:(j + 1) * LANES] = (
                    jnp.where(lane_q < SWA_HEAD_DIM, lo, hi).astype(BF16))


def _swa_attention(q, kv, sink_rows, *, tq=512, pair=2):
    b, s, _ = q.shape
    per = tq // BLOCK
    cur = lambda bi, i: (bi, i, 0)
    prev = lambda bi, i: (bi, jnp.maximum(i * per - 1, 0), 0)
    cur_v = lambda bi, i: (bi, i, 1)
    prev_v = lambda bi, i: (bi, jnp.maximum(i * per - 1, 0), 1)
    rows = sink_rows.shape[0]
    return pl.pallas_call(
        functools.partial(_swa_kernel, tq=tq, pair=pair),
        out_shape=jax.ShapeDtypeStruct((b, s, D_MODEL), BF16),
        grid=(b, s // tq),
        in_specs=[
            pl.BlockSpec((rows, LANES), lambda bi, i: (0, 0)),
            pl.BlockSpec((1, tq, D_MODEL), cur),
            pl.BlockSpec((1, BLOCK, LANES), prev),
            pl.BlockSpec((1, tq, LANES), cur),
            pl.BlockSpec((1, BLOCK, LANES), prev_v),
            pl.BlockSpec((1, tq, LANES), cur_v),
        ],
        out_specs=pl.BlockSpec((1, tq, D_MODEL), cur),
        scratch_shapes=[pltpu.VMEM((2, BLOCK, 2 * BLOCK), F32)],
        compiler_params=_params("arbitrary", "arbitrary"),
        name="swa_attention",
    )(sink_rows, q, kv, kv, kv, kv)


def _paired_head_order():
    cols = []
    for j in range(SWA_GROUP):
        cols.extend(range(j * SWA_HEAD_DIM, (j + 1) * SWA_HEAD_DIM))
        cols.extend(range((SWA_GROUP + j) * SWA_HEAD_DIM, (SWA_GROUP + j + 1) * SWA_HEAD_DIM))
    return jnp.asarray(cols, dtype=jnp.int32)


def kernel(x, positions, ln_ffn1, ffn1_w_gate, ffn1_w_up, ffn1_w_down, ln_mix, ln_ffn2, ffn2_w_gate, ffn2_w_up, ffn2_w_down, a_w_qkv, a_w_o, a_lambda_q1, a_lambda_k1, a_lambda_q2, a_lambda_k2, a_subln, b_w_q, b_b_q, b_sinks, b_w_o, b_b_o, kv_norm, kv_w_k, kv_b_k, kv_w_v, kv_b_v, final_norm):
    batch, seq, _ = x.shape
    n = batch * seq
    tables = _rope_tables(positions)
    h = x.reshape(n, D_MODEL)
    perm = _paired_head_order()
    kv = None
    for layer in range(DEPTH):
        if layer < N_A_LAYERS:
            a = layer
            proj = (ln_mix[layer], a_w_qkv[a].astype(BF16), jnp.zeros((3 * D_MODEL,), F32), tables,
                    dict(rope_cols=2 * D_MODEL, scale_cols=D_MODEL,
                         scale=DIFF_SUB_DIM ** -0.5 * LOG2_E, head_major=True))
        else:
            b = layer - N_A_LAYERS
            proj = (ln_mix[layer], b_w_q[b][:, perm].astype(BF16), b_b_q[b][perm], tables,
                    dict(rope_cols=D_MODEL, scale_cols=D_MODEL,
                         scale=SWA_HEAD_DIM ** -0.5 * LOG2_E, head_major=False))
        h, mixer_in = _ffn(h, ln_ffn1[layer], ffn1_w_gate[layer].astype(BF16),
                           ffn1_w_up[layer].astype(BF16), ffn1_w_down[layer].astype(BF16), proj=proj)
        if layer < N_A_LAYERS:
            lambda_init = 0.8 - 0.6 * math.exp(-0.3 * layer)
            lam_vecs = jnp.stack([a_lambda_q1[a], a_lambda_k1[a], a_lambda_q2[a], a_lambda_k2[a]])
            o = _diff_attention(mixer_in.reshape(3 * DIFF_HEADS, batch, seq, DIFF_V_DIM), lam_vecs,
                                a_subln[a], lambda_init)
            attn = (o.reshape(n, D_MODEL), a_w_o[a].astype(BF16), jnp.zeros((D_MODEL,), F32))
        else:
            sink_pairs = jnp.stack([b_sinks[b][:SWA_GROUP], b_sinks[b][SWA_GROUP:]], axis=1)
            sink_rows = jnp.broadcast_to(
                jnp.repeat(sink_pairs.reshape(-1) * LOG2_E, BLOCK)[:, None], (2 * D_MODEL, LANES))
            o = _swa_attention(mixer_in.reshape(batch, seq, D_MODEL), kv, sink_rows)
            attn = (o.reshape(n, D_MODEL), b_w_o[b][perm, :].astype(BF16), b_b_o[b])
        proj = None
        if layer == N_A_LAYERS - 1:
            proj = (kv_norm, jnp.concatenate([kv_w_k, kv_w_v], axis=1).astype(BF16),
                    jnp.concatenate([kv_b_k, kv_b_v]), tables,
                    dict(rope_cols=SWA_KV_WIDTH, scale_cols=0, scale=1.0, head_major=False))
        res = _ffn(h, ln_ffn2[layer], ffn2_w_gate[layer].astype(BF16), ffn2_w_up[layer].astype(BF16),
                   ffn2_w_down[layer].astype(BF16), attn=attn, proj=proj,
                   final_g=final_norm if layer == DEPTH - 1 else None)
        if proj is None:
            h = res
        else:
            h, kv = res[0], res[1].reshape(batch, seq, 2 * SWA_KV_WIDTH)
    return h.reshape(batch, seq, D_MODEL)
```

```python
import functools
import math

import jax
import jax.numpy as jnp
from jax import lax
from jax.experimental import pallas as pl
from jax.experimental.pallas import tpu as pltpu

D_MODEL = 1024
D_FF = 2816
DEPTH = 4
N_A_LAYERS = 2
DIFF_HEADS = 8
DIFF_SUB_DIM = 64
DIFF_V_DIM = 128
SWA_Q_HEADS = 16
SWA_KV_HEADS = 2
SWA_HEAD_DIM = 64
SWA_GROUP = SWA_Q_HEADS // SWA_KV_HEADS
SWA_KV_WIDTH = SWA_KV_HEADS * SWA_HEAD_DIM
WINDOW = 128
BLOCK = 128
ROPE_THETA = 500000.0
ROT_DIM = 16
ROT_HALF = ROT_DIM // 2
NORM_EPS = 1e-5

LANES = 128
VMEM_LIMIT = 56 * 1024 * 1024
VMEM_COMPILER_RESERVE = 6 * 1024 * 1024
MASK_VALUE = -1e30
LOG2_E = math.log2(math.e)

F32 = jnp.float32
BF16 = jnp.bfloat16


def _params(*sem):
    return pltpu.CompilerParams(dimension_semantics=sem, vmem_limit_bytes=VMEM_LIMIT)


def _resident(shape):
    return pl.BlockSpec(shape, lambda *_: (0,) * len(shape), pipeline_mode=pl.Buffered(1))


def _rms_norm(x, g):
    ms = jnp.mean(x * x, axis=-1, keepdims=True)
    return x * lax.rsqrt(ms + NORM_EPS) * g


def _rope_table_kernel(pos_ref, invf_ref, c_ref, s1_ref, s2_ref):
    ang = pos_ref[...] * invf_ref[...]
    cos = jnp.cos(ang)
    sin = jnp.sin(ang)
    lane = lax.broadcasted_iota(jnp.int32, ang.shape, 1) % SWA_HEAD_DIM
    first = lane < ROT_HALF
    second = (lane >= ROT_HALF) & (lane < ROT_DIM)
    c_ref[...] = jnp.where(first | second, cos, 1.0)
    s1_ref[...] = jnp.where(first, -sin, 0.0)
    s2_ref[...] = jnp.where(second, sin, 0.0)


def _rope_tables(positions, tm=1024):
    n = positions.size
    pos = jnp.broadcast_to(positions.reshape(n, 1).astype(F32), (n, LANES))
    inv_freq = ROPE_THETA ** (-jnp.arange(0, ROT_DIM, 2, dtype=F32) / ROT_DIM)
    invf = jnp.tile(inv_freq, LANES // ROT_HALF).reshape(1, LANES)
    spec = pl.BlockSpec((tm, LANES), lambda i: (i, 0))
    out = jax.ShapeDtypeStruct((n, LANES), F32)
    return pl.pallas_call(
        _rope_table_kernel,
        out_shape=(out, out, out),
        grid=(n // tm,),
        in_specs=[spec, pl.BlockSpec((1, LANES), lambda i: (0, 0))],
        out_specs=(spec, spec, spec),
        compiler_params=_params("arbitrary"),
        name="rope_tables",
    )(pos, invf)


def _apply_rope(x, c, s1, s2):
    return (x * c + pltpu.roll(x, LANES - ROT_HALF, 1) * s1 + pltpu.roll(x, ROT_HALF, 1) * s2)


def _project(xn_ref, w_ref, b_ref, tables, o_ref, *, rope_cols, scale_cols, scale, head_major):
    n_out = w_ref.shape[1]
    chunk = min(n_out, 512)
    c, s1, s2 = tables
    for c0 in range(0, n_out, chunk):
        cols = slice(c0, c0 + chunk)
        y = jnp.dot(xn_ref[...], w_ref[:, cols], preferred_element_type=F32) + b_ref[:, cols]
        for j in range(0, chunk, LANES):
            yj = y[:, j:j + LANES]
            if c0 + j < rope_cols:
                yj = _apply_rope(yj, c, s1, s2)
            if c0 + j < scale_cols:
                yj = yj * scale
            if head_major:
                o_ref[(c0 + j) // LANES] = yj.astype(BF16)
            else:
                o_ref[:, c0 + j:c0 + j + LANES] = yj.astype(BF16)


def _ffn_kernel(*refs, tf, has_attn, has_final, proj):
    h_ref, g_ref, wg_ref, wu_ref, wd_ref = refs[:5]
    rest = list(refs[5:])
    if has_attn:
        a_ref, wo_ref, bo_ref = rest[:3]
        rest = rest[3:]
    if has_final:
        gf_ref = rest.pop(0)
    if proj is not None:
        gp_ref, wp_ref, bp_ref, c_ref, s1_ref, s2_ref = rest[:6]
        rest = rest[6:]
        p_ref = rest.pop(1)
    o_ref, xn_ref, acc_ref = rest[:3]
    h = h_ref[...]
    if has_attn:
        res_ref = rest[3]
        h = h + bo_ref[...] + jnp.dot(a_ref[...], wo_ref[...], preferred_element_type=F32)
        res_ref[...] = h
    else:
        res_ref = h_ref
    xn_ref[...] = _rms_norm(h, g_ref[...]).astype(BF16)
    for c in range(D_FF // tf):
        cols = slice(c * tf, (c + 1) * tf)
        xn = xn_ref[...]
        gate = jnp.dot(xn, wg_ref[:, cols], preferred_element_type=F32)
        up = jnp.dot(xn, wu_ref[:, cols], preferred_element_type=F32)
        act = (gate / (1.0 + jnp.exp(-gate)) * up).astype(BF16)
        down = jnp.dot(act, wd_ref[cols, :], preferred_element_type=F32)
        if c == 0:
            acc_ref[...] = down
        else:
            acc_ref[...] += down
    out = res_ref[...] + 0.5 * acc_ref[...]
    if has_final:
        out = _rms_norm(out, gf_ref[...])
    o_ref[...] = out
    if proj is not None:
        xn_ref[...] = _rms_norm(out, gp_ref[...]).astype(BF16)
        _project(xn_ref, wp_ref, bp_ref, (c_ref[...], s1_ref[...], s2_ref[...]), p_ref, **proj)


def _ffn_row_tile(has_attn, proj_cols):
    weights = 2 * (3 * D_MODEL * D_FF + (D_MODEL * D_MODEL if has_attn else 0)
                   + D_MODEL * proj_cols)
    per_row = 2 * 4 * D_MODEL * 2 + 2 * D_MODEL + 4 * D_MODEL
    if has_attn:
        per_row += 2 * 2 * D_MODEL + 4 * D_MODEL
    if proj_cols:
        per_row += 2 * 2 * proj_cols + 2 * 3 * 4 * LANES
    for tm in (1024, 512):
        if weights + tm * per_row <= VMEM_LIMIT - VMEM_COMPILER_RESERVE:
            return tm
    raise ValueError("FFN blocks do not fit in VMEM")


def _ffn(h, g, wg, wu, wd, *, attn=None, final_g=None, proj=None, tf=256):
    n = h.shape[0]
    tm = _ffn_row_tile(attn is not None, 0 if proj is None else proj[1].shape[1])
    row = pl.BlockSpec((tm, D_MODEL), lambda i: (i, 0))
    vec = _resident((1, D_MODEL))
    args = [h, g.reshape(1, D_MODEL), wg, wu, wd]
    specs = [row, vec, _resident((D_MODEL, D_FF)), _resident((D_MODEL, D_FF)),
             _resident((D_FF, D_MODEL))]
    scratch = [pltpu.VMEM((tm, D_MODEL), BF16), pltpu.VMEM((tm, D_MODEL), F32)]
    out_shape = jax.ShapeDtypeStruct((n, D_MODEL), F32)
    out_specs = row
    if attn is not None:
        a, wo, bo = attn
        args += [a, wo, bo.reshape(1, D_MODEL)]
        specs += [row, _resident((D_MODEL, D_MODEL)), vec]
        scratch.append(pltpu.VMEM((tm, D_MODEL), F32))
    if final_g is not None:
        args.append(final_g.reshape(1, D_MODEL))
        specs.append(vec)
    options = None
    if proj is not None:
        gp, wp, bp, tables, options = proj
        n_out = wp.shape[1]
        assert (options["rope_cols"] % LANES == 0 and options["scale_cols"] % LANES == 0
                and n_out % min(n_out, 512) == 0)
        tab = pl.BlockSpec((tm, LANES), lambda i: (i, 0))
        args += [gp.reshape(1, D_MODEL), wp, bp.reshape(1, n_out), *tables]
        specs += [vec, _resident((D_MODEL, n_out)), _resident((1, n_out)), tab, tab, tab]
        if options["head_major"]:
            p_shape = jax.ShapeDtypeStruct((n_out // LANES, n, LANES), BF16)
            p_spec = pl.BlockSpec((n_out // LANES, tm, LANES), lambda i: (0, i, 0))
        else:
            p_shape = jax.ShapeDtypeStruct((n, n_out), BF16)
            p_spec = pl.BlockSpec((tm, n_out), lambda i: (i, 0))
        out_shape = (out_shape, p_shape)
        out_specs = (row, p_spec)
    return pl.pallas_call(
        functools.partial(_ffn_kernel, tf=tf, has_attn=attn is not None,
                          has_final=final_g is not None, proj=options),
        out_shape=out_shape,
        grid=(n // tm,),
        in_specs=specs,
        out_specs=out_specs,
        scratch_shapes=scratch,
        compiler_params=_params("arbitrary"),
        name="ffn",
    )(*args)


def _diff_attn_kernel(lam_ref, gsub_ref, q_ref, k_ref, v_ref, o_ref, s_ref, m_ref, acc_ref, *,
                      t, hp, lambda_init):
    i = pl.program_id(2)
    lane = lax.broadcasted_iota(jnp.int32, (t, LANES), 1)
    ones = jnp.ones((t, LANES), BF16)
    q2 = []
    for hh in range(hp):
        q = q_ref[hh, 0]
        zero = jnp.zeros_like(q)
        q2.append(jnp.concatenate([jnp.where(lane < DIFF_SUB_DIM, q, zero),
                                   jnp.where(lane >= DIFF_SUB_DIM, q, zero)], axis=0))

    def scores(hh, j):
        k = k_ref[hh, 0, pl.ds(j * t, t), :]
        return lax.dot_general(q2[hh], k, (((1,), (1,)), ((), ())), preferred_element_type=F32)

    def attend(hh, j, masked):
        s = s_ref[hh]
        if masked:
            row = lax.broadcasted_iota(jnp.int32, s.shape, 0) % t
            col = lax.broadcasted_iota(jnp.int32, s.shape, 1)
            s = jnp.where(col <= row, s, MASK_VALUE)
        m_prev = m_ref[hh]
        m_next = jnp.maximum(m_prev, jnp.max(s, axis=-1, keepdims=True))
        m_ref[hh] = m_next
        alpha = jnp.tile(jnp.exp2(m_prev - m_next), (1, 2))
        p = jnp.exp2(s - jnp.tile(m_next, (1, t // LANES))).astype(BF16)
        v_aug = jnp.concatenate([v_ref[hh, 0, pl.ds(j * t, t), :], ones], axis=1)
        return alpha * acc_ref[hh] + jnp.dot(p, v_aug, preferred_element_type=F32)

    m_ref[...] = jnp.full(m_ref.shape, MASK_VALUE, F32)
    acc_ref[...] = jnp.zeros(acc_ref.shape, F32)
    for hh in range(hp):
        s_ref[hh] = scores(hh, 0)

    def body(j, carry):
        for hh in range(hp):
            acc = attend(hh, j, False)
            s_ref[hh] = scores(hh, j + 1)
            acc_ref[hh] = acc
        return carry

    lax.fori_loop(0, i, body, 0)

    lv = lam_ref[...]
    lam = (jnp.exp(jnp.sum(lv[0:1] * lv[1:2], axis=-1, keepdims=True))
           - jnp.exp(jnp.sum(lv[2:3] * lv[3:4], axis=-1, keepdims=True)) + lambda_init)
    for hh in range(hp):
        acc = attend(hh, i, True)
        attn = acc[:, :DIFF_V_DIM] / acc[:, DIFF_V_DIM:]
        o = attn[:t] - lam * attn[t:]
        o = _rms_norm(o, gsub_ref[...]) * (1.0 - lambda_init)
        o_ref[0, :, hh * DIFF_V_DIM:(hh + 1) * DIFF_V_DIM] = o.astype(BF16)


def _diff_attention(qkv, lam_vecs, g_sub, lambda_init, *, t=512, hp=4):
    _, b, s, _ = qkv.shape
    hg = DIFF_HEADS // hp
    return pl.pallas_call(
        functools.partial(_diff_attn_kernel, t=t, hp=hp, lambda_init=lambda_init),
        out_shape=jax.ShapeDtypeStruct((b, s, D_MODEL), BF16),
        grid=(b, hg, s // t),
        in_specs=[
            pl.BlockSpec((4, DIFF_SUB_DIM), lambda bi, hi, i: (0, 0)),
            pl.BlockSpec((1, DIFF_V_DIM), lambda bi, hi, i: (0, 0)),
            pl.BlockSpec((hp, 1, t, DIFF_V_DIM), lambda bi, hi, i: (hi, bi, i, 0)),
            pl.BlockSpec((hp, 1, s, DIFF_V_DIM), lambda bi, hi, i: (hg + hi, bi, 0, 0)),
            pl.BlockSpec((hp, 1, s, DIFF_V_DIM), lambda bi, hi, i: (2 * hg + hi, bi, 0, 0)),
        ],
        out_specs=pl.BlockSpec((1, t, hp * DIFF_V_DIM), lambda bi, hi, i: (bi, i, hi)),
        scratch_shapes=[pltpu.VMEM((hp, 2 * t, t), F32), pltpu.VMEM((hp, 2 * t, LANES), F32),
                        pltpu.VMEM((hp, 2 * t, 2 * DIFF_V_DIM), F32)],
        compiler_params=_params("arbitrary", "arbitrary", "arbitrary"),
        name="diff_attention",
    )(lam_vecs, g_sub.reshape(1, DIFF_V_DIM), qkv, qkv, qkv)


def _swa_kernel(sink_ref, q_ref, kp_ref, kc_ref, vp_ref, vc_ref, o_ref, bias_ref, *, tq, pair):
    i = pl.program_id(1)
    kk = jnp.concatenate([kp_ref[0], kc_ref[0]], axis=0)
    vv = jnp.concatenate([vp_ref[0], vc_ref[0]], axis=0)
    ones = jnp.ones((2 * BLOCK, LANES), BF16)
    rows = 2 * pair * BLOCK
    lane_q = lax.broadcasted_iota(jnp.int32, (BLOCK, LANES), 1)
    qi = lax.broadcasted_iota(jnp.int32, (BLOCK, 2 * BLOCK), 0)
    kj = lax.broadcasted_iota(jnp.int32, (BLOCK, 2 * BLOCK), 1)
    in_window = (kj > qi) & (kj <= qi + WINDOW)
    bias_ref[0] = jnp.where(in_window, 0.0, MASK_VALUE)
    bias_ref[1] = jnp.where(in_window & (kj >= jnp.where(i > 0, 0, BLOCK)), 0.0, MASK_VALUE)
    for c in range(tq // BLOCK):
        blk = slice(c * BLOCK, (c + 1) * BLOCK)
        k = kk[c * BLOCK:(c + 2) * BLOCK]
        v_aug = jnp.concatenate([vv[c * BLOCK:(c + 2) * BLOCK], ones], axis=1)
        for jc in range(D_MODEL // LANES // pair):
            parts = []
            for j in range(jc * pair, (jc + 1) * pair):
                qj = q_ref[0, blk, j * LANES:(j + 1) * LANES]
                parts.append(jnp.where(lane_q < SWA_HEAD_DIM, qj, jnp.zeros_like(qj)))
                parts.append(jnp.where(lane_q >= SWA_HEAD_DIM, qj, jnp.zeros_like(qj)))
            q2 = jnp.concatenate(parts, axis=0)
            s = lax.dot_general(q2, k, (((1,), (1,)), ((), ())), preferred_element_type=F32)
            s = s + jnp.tile(bias_ref[1 if c == 0 else 0], (rows // BLOCK, 1))
            sink = sink_ref[jc * rows:(jc + 1) * rows, :]
            m = jnp.maximum(jnp.max(s, axis=-1, keepdims=True), sink)
            p = jnp.exp2(s - jnp.tile(m, (1, 2))).astype(BF16)
            pv = jnp.dot(p, v_aug, preferred_element_type=F32)
            o2 = pv[:, :LANES] / (pv[:, LANES:] + jnp.exp2(sink - m))
            for jj in range(pair):
                j = jc * pair + jj
                lo = o2[(2 * jj) * BLOCK:(2 * jj + 1) * BLOCK]
                hi = o2[(2 * jj + 1) * BLOCK:(2 * jj + 2) * BLOCK]
                o_ref[0, blk, j * LANES:(j + 1) * LANES] = (
                    jnp.where(lane_q < SWA_HEAD_DIM, lo, hi).astype(BF16))


def _swa_attention(q, kv, sink_rows, *, tq=512, pair=2):
    b, s, _ = q.shape
    per = tq // BLOCK
    cur = lambda bi, i: (bi, i, 0)
    prev = lambda bi, i: (bi, jnp.maximum(i * per - 1, 0), 0)
    cur_v = lambda bi, i: (bi, i, 1)
    prev_v = lambda bi, i: (bi, jnp.maximum(i * per - 1, 0), 1)
    rows = sink_rows.shape[0]
    return pl.pallas_call(
        functools.partial(_swa_kernel, tq=tq, pair=pair),
        out_shape=jax.ShapeDtypeStruct((b, s, D_MODEL), BF16),
        grid=(b, s // tq),
        in_specs=[
            pl.BlockSpec((rows, LANES), lambda bi, i: (0, 0)),
            pl.BlockSpec((1, tq, D_MODEL), cur),
            pl.BlockSpec((1, BLOCK, LANES), prev),
            pl.BlockSpec((1, tq, LANES), cur),
            pl.BlockSpec((1, BLOCK, LANES), prev_v),
            pl.BlockSpec((1, tq, LANES), cur_v),
        ],
        out_specs=pl.BlockSpec((1, tq, D_MODEL), cur),
        scratch_shapes=[pltpu.VMEM((2, BLOCK, 2 * BLOCK), F32)],
        compiler_params=_params("arbitrary", "arbitrary"),
        name="swa_attention",
    )(sink_rows, q, kv, kv, kv, kv)


def _paired_head_order():
    cols = []
    for j in range(SWA_GROUP):
        cols.extend(range(j * SWA_HEAD_DIM, (j + 1) * SWA_HEAD_DIM))
        cols.extend(range((SWA_GROUP + j) * SWA_HEAD_DIM, (SWA_GROUP + j + 1) * SWA_HEAD_DIM))
    return jnp.asarray(cols, dtype=jnp.int32)


def kernel(x, positions, ln_ffn1, ffn1_w_gate, ffn1_w_up, ffn1_w_down, ln_mix, ln_ffn2, ffn2_w_gate, ffn2_w_up, ffn2_w_down, a_w_qkv, a_w_o, a_lambda_q1, a_lambda_k1, a_lambda_q2, a_lambda_k2, a_subln, b_w_q, b_b_q, b_sinks, b_w_o, b_b_o, kv_norm, kv_w_k, kv_b_k, kv_w_v, kv_b_v, final_norm):
    batch, seq, _ = x.shape
    n = batch * seq
    tables = _rope_tables(positions)
    h = x.reshape(n, D_MODEL)
    perm = _paired_head_order()
    kv = None
    for layer in range(DEPTH):
        if layer < N_A_LAYERS:
            a = layer
            proj = (ln_mix[layer], a_w_qkv[a].astype(BF16), jnp.zeros((3 * D_MODEL,), F32), tables,
                    dict(rope_cols=2 * D_MODEL, scale_cols=D_MODEL,
                         scale=DIFF_SUB_DIM ** -0.5 * LOG2_E, head_major=True))
        else:
            b = layer - N_A_LAYERS
            proj = (ln_mix[layer], b_w_q[b][:, perm].astype(BF16), b_b_q[b][perm], tables,
                    dict(rope_cols=D_MODEL, scale_cols=D_MODEL,
                         scale=SWA_HEAD_DIM ** -0.5 * LOG2_E, head_major=False))
        h, mixer_in = _ffn(h, ln_ffn1[layer], ffn1_w_gate[layer].astype(BF16),
                           ffn1_w_up[layer].astype(BF16), ffn1_w_down[layer].astype(BF16), proj=proj)
        if layer < N_A_LAYERS:
            lambda_init = 0.8 - 0.6 * math.exp(-0.3 * layer)
            lam_vecs = jnp.stack([a_lambda_q1[a], a_lambda_k1[a], a_lambda_q2[a], a_lambda_k2[a]])
            o = _diff_attention(mixer_in.reshape(3 * DIFF_HEADS, batch, seq, DIFF_V_DIM), lam_vecs,
                                a_subln[a], lambda_init)
            attn = (o.reshape(n, D_MODEL), a_w_o[a].astype(BF16), jnp.zeros((D_MODEL,), F32))
        else:
            sink_pairs = jnp.stack([b_sinks[b][:SWA_GROUP], b_sinks[b][SWA_GROUP:]], axis=1)
            sink_rows = jnp.broadcast_to(
                jnp.repeat(sink_pairs.reshape(-1) * LOG2_E, BLOCK)[:, None], (2 * D_MODEL, LANES))
            o = _swa_attention(mixer_in.reshape(batch, seq, D_MODEL), kv, sink_rows)
            attn = (o.reshape(n, D_MODEL), b_w_o[b][perm, :].astype(BF16), b_b_o[b])
        proj = None
        if layer == N_A_LAYERS - 1:
            proj = (kv_norm, jnp.concatenate([kv_w_k, kv_w_v], axis=1).astype(BF16),
                    jnp.concatenate([kv_b_k, kv_b_v]), tables,
                    dict(rope_cols=SWA_KV_WIDTH, scale_cols=0, scale=1.0, head_major=False))
        res = _ffn(h, ln_ffn2[layer], ffn2_w_gate[layer].astype(BF16), ffn2_w_up[layer].astype(BF16),
                   ffn2_w_down[layer].astype(BF16), attn=attn, proj=proj,
                   final_g=final_norm if layer == DEPTH - 1 else None)
        if proj is None:
            h = res
        else:
            h, kv = res[0], res[1].reshape(batch, seq, 2 * SWA_KV_WIDTH)
    return h.reshape(batch, seq, D_MODEL)
```

```python
import functools
import math

import jax
import jax.numpy as jnp
from jax import lax
from jax.experimental import pallas as pl
from jax.experimental.pallas import tpu as pltpu

D_MODEL = 1024
D_FF = 2816
DEPTH = 4
N_A_LAYERS = 2
DIFF_HEADS = 8
DIFF_SUB_DIM = 64
DIFF_V_DIM = 128
SWA_Q_HEADS = 16
SWA_KV_HEADS = 2
SWA_HEAD_DIM = 64
SWA_GROUP = SWA_Q_HEADS // SWA_KV_HEADS
SWA_KV_WIDTH = SWA_KV_HEADS * SWA_HEAD_DIM
WINDOW = 128
BLOCK = 128
ROPE_THETA = 500000.0
ROT_DIM = 16
ROT_HALF = ROT_DIM // 2
NORM_EPS = 1e-5

LANES = 128
VMEM_LIMIT = 56 * 1024 * 1024
VMEM_COMPILER_RESERVE = 6 * 1024 * 1024
MASK_VALUE = -1e30
LOG2_E = math.log2(math.e)

F32 = jnp.float32
BF16 = jnp.bfloat16


def _params(*sem):
    return pltpu.CompilerParams(dimension_semantics=sem, vmem_limit_bytes=VMEM_LIMIT)


def _resident(shape):
    return pl.BlockSpec(shape, lambda *_: (0,) * len(shape), pipeline_mode=pl.Buffered(1))


def _rms_norm(x, g):
    ms = jnp.mean(x * x, axis=-1, keepdims=True)
    return x * lax.rsqrt(ms + NORM_EPS) * g


def _rope_table_kernel(pos_ref, invf_ref, c_ref, s1_ref, s2_ref):
    ang = pos_ref[...] * invf_ref[...]
    cos = jnp.cos(ang)
    sin = jnp.sin(ang)
    lane = lax.broadcasted_iota(jnp.int32, ang.shape, 1) % SWA_HEAD_DIM
    first = lane < ROT_HALF
    second = (lane >= ROT_HALF) & (lane < ROT_DIM)
    c_ref[...] = jnp.where(first | second, cos, 1.0)
    s1_ref[...] = jnp.where(first, -sin, 0.0)
    s2_ref[...] = jnp.where(second, sin, 0.0)


def _rope_tables(positions, tm=1024):
    n = positions.size
    pos = jnp.broadcast_to(positions.reshape(n, 1).astype(F32), (n, LANES))
    inv_freq = ROPE_THETA ** (-jnp.arange(0, ROT_DIM, 2, dtype=F32) / ROT_DIM)
    invf = jnp.tile(inv_freq, LANES // ROT_HALF).reshape(1, LANES)
    spec = pl.BlockSpec((tm, LANES), lambda i: (i, 0))
    out = jax.ShapeDtypeStruct((n, LANES), F32)
    return pl.pallas_call(
        _rope_table_kernel,
        out_shape=(out, out, out),
        grid=(n // tm,),
        in_specs=[spec, pl.BlockSpec((1, LANES), lambda i: (0, 0))],
        out_specs=(spec, spec, spec),
        compiler_params=_params("arbitrary"),
        name="rope_tables",
    )(pos, invf)


def _apply_rope(x, c, s1, s2):
    return (x * c + pltpu.roll(x, LANES - ROT_HALF, 1) * s1 + pltpu.roll(x, ROT_HALF, 1) * s2)


def _project(xn_ref, w_ref, b_ref, tables, o_ref, *, rope_cols, scale_cols, scale, head_major):
    n_out = w_ref.shape[1]
    chunk = min(n_out, 512)
    c, s1, s2 = tables
    for c0 in range(0, n_out, chunk):
        cols = slice(c0, c0 + chunk)
        y = jnp.dot(xn_ref[...], w_ref[:, cols], preferred_element_type=F32) + b_ref[:, cols]
        for j in range(0, chunk, LANES):
            yj = y[:, j:j + LANES]
            if c0 + j < rope_cols:
                yj = _apply_rope(yj, c, s1, s2)
            if c0 + j < scale_cols:
                yj = yj * scale
            if head_major:
                o_ref[(c0 + j) // LANES] = yj.astype(BF16)
            else:
                o_ref[:, c0 + j:c0 + j + LANES] = yj.astype(BF16)


def _ffn_kernel(*refs, tf, has_attn, has_final, proj):
    h_ref, g_ref, wg_ref, wu_ref, wd_ref = refs[:5]
    rest = list(refs[5:])
    if has_attn:
        a_ref, wo_ref, bo_ref = rest[:3]
        rest = rest[3:]
    if has_final:
        gf_ref = rest.pop(0)
    if proj is not None:
        gp_ref, wp_ref, bp_ref, c_ref, s1_ref, s2_ref = rest[:6]
        rest = rest[6:]
        p_ref = rest.pop(1)
    o_ref, xn_ref, acc_ref = rest[:3]
    h = h_ref[...]
    if has_attn:
        res_ref = rest[3]
        h = h + bo_ref[...] + jnp.dot(a_ref[...], wo_ref[...], preferred_element_type=F32)
        res_ref[...] = h
    else:
        res_ref = h_ref
    xn_ref[...] = _rms_norm(h, g_ref[...]).astype(BF16)
    for c in range(D_FF // tf):
        cols = slice(c * tf, (c + 1) * tf)
        xn = xn_ref[...]
        gate = jnp.dot(xn, wg_ref[:, cols], preferred_element_type=F32)
        up = jnp.dot(xn, wu_ref[:, cols], preferred_element_type=F32)
        act = (gate / (1.0 + jnp.exp(-gate)) * up).astype(BF16)
        down = jnp.dot(act, wd_ref[cols, :], preferred_element_type=F32)
        if c == 0:
            acc_ref[...] = down
        else:
            acc_ref[...] += down
    out = res_ref[...] + 0.5 * acc_ref[...]
    if has_final:
        out = _rms_norm(out, gf_ref[...])
    o_ref[...] = out
    if proj is not None:
        xn_ref[...] = _rms_norm(out, gp_ref[...]).astype(BF16)
        _project(xn_ref, wp_ref, bp_ref, (c_ref[...], s1_ref[...], s2_ref[...]), p_ref, **proj)


def _ffn_row_tile(has_attn, proj_cols):
    weights = 2 * (3 * D_MODEL * D_FF + (D_MODEL * D_MODEL if has_attn else 0)
                   + D_MODEL * proj_cols)
    per_row = 2 * 4 * D_MODEL * 2 + 2 * D_MODEL + 4 * D_MODEL
    if has_attn:
        per_row += 2 * 2 * D_MODEL + 4 * D_MODEL
    if proj_cols:
        per_row += 2 * 2 * proj_cols + 2 * 3 * 4 * LANES
    for tm in (1024, 512):
        if weights + tm * per_row <= VMEM_LIMIT - VMEM_COMPILER_RESERVE:
            return tm
    raise ValueError("FFN blocks do not fit in VMEM")


def _ffn(h, g, wg, wu, wd, *, attn=None, final_g=None, proj=None, tf=256):
    n = h.shape[0]
    tm = _ffn_row_tile(attn is not None, 0 if proj is None else proj[1].shape[1])
    row = pl.BlockSpec((tm, D_MODEL), lambda i: (i, 0))
    vec = _resident((1, D_MODEL))
    args = [h, g.reshape(1, D_MODEL), wg, wu, wd]
    specs = [row, vec, _resident((D_MODEL, D_FF)), _resident((D_MODEL, D_FF)),
             _resident((D_FF, D_MODEL))]
    scratch = [pltpu.VMEM((tm, D_MODEL), BF16), pltpu.VMEM((tm, D_MODEL), F32)]
    out_shape = jax.ShapeDtypeStruct((n, D_MODEL), F32)
    out_specs = row
    if attn is not None:
        a, wo, bo = attn
        args += [a, wo, bo.reshape(1, D_MODEL)]
        specs += [row, _resident((D_MODEL, D_MODEL)), vec]
        scratch.append(pltpu.VMEM((tm, D_MODEL), F32))
    if final_g is not None:
        args.append(final_g.reshape(1, D_MODEL))
        specs.append(vec)
    options = None
    if proj is not None:
        gp, wp, bp, tables, options = proj
        n_out = wp.shape[1]
        assert (options["rope_cols"] % LANES == 0 and options["scale_cols"] % LANES == 0
                and n_out % min(n_out, 512) == 0)
        tab = pl.BlockSpec((tm, LANES), lambda i: (i, 0))
        args += [gp.reshape(1, D_MODEL), wp, bp.reshape(1, n_out), *tables]
        specs += [vec, _resident((D_MODEL, n_out)), _resident((1, n_out)), tab, tab, tab]
        if options["head_major"]:
            p_shape = jax.ShapeDtypeStruct((n_out // LANES, n, LANES), BF16)
            p_spec = pl.BlockSpec((n_out // LANES, tm, LANES), lambda i: (0, i, 0))
        else:
            p_shape = jax.ShapeDtypeStruct((n, n_out), BF16)
            p_spec = pl.BlockSpec((tm, n_out), lambda i: (i, 0))
        out_shape = (out_shape, p_shape)
        out_specs = (row, p_spec)
    return pl.pallas_call(
        functools.partial(_ffn_kernel, tf=tf, has_attn=attn is not None,
                          has_final=final_g is not None, proj=options),
        out_shape=out_shape,
        grid=(n // tm,),
        in_specs=specs,
        out_specs=out_specs,
        scratch_shapes=scratch,
        compiler_params=_params("arbitrary"),
        name="ffn",
    )(*args)


def _diff_attn_kernel(lam_ref, gsub_ref, q_ref, qn_ref, k_ref, v_ref, o_ref, s_ref, m_ref, acc_ref,
                      *, t, hp, lambda_init):
    i = pl.program_id(2)
    lane = lax.broadcasted_iota(jnp.int32, (t, LANES), 1)
    ones = jnp.ones((t, LANES), BF16)

    def stacked(ref, hh):
        q = ref[hh, 0]
        zero = jnp.zeros_like(q)
        return jnp.concatenate([jnp.where(lane < DIFF_SUB_DIM, q, zero),
                                jnp.where(lane >= DIFF_SUB_DIM, q, zero)], axis=0)

    q2 = [stacked(q_ref, hh) for hh in range(hp)]

    def scores_of(q_stacked, hh, j):
        k = k_ref[hh, 0, pl.ds(j * t, t), :]
        return lax.dot_general(q_stacked, k, (((1,), (1,)), ((), ())), preferred_element_type=F32)

    def scores(hh, j):
        return scores_of(q2[hh], hh, j)

    def attend(hh, j, masked):
        s = s_ref[hh]
        if masked:
            row = lax.broadcasted_iota(jnp.int32, s.shape, 0) % t
            col = lax.broadcasted_iota(jnp.int32, s.shape, 1)
            s = jnp.where(col <= row, s, MASK_VALUE)
        m_prev = m_ref[hh]
        m_next = jnp.maximum(m_prev, jnp.max(s, axis=-1, keepdims=True))
        m_ref[hh] = m_next
        alpha = jnp.tile(jnp.exp2(m_prev - m_next), (1, 2))
        p = jnp.exp2(s - jnp.tile(m_next, (1, t // LANES))).astype(BF16)
        v_aug = jnp.concatenate([v_ref[hh, 0, pl.ds(j * t, t), :], ones], axis=1)
        return alpha * acc_ref[hh] + jnp.dot(p, v_aug, preferred_element_type=F32)

    @pl.when(i == 0)
    def _():
        m_ref[...] = jnp.full(m_ref.shape, MASK_VALUE, F32)
        acc_ref[...] = jnp.zeros(acc_ref.shape, F32)
        for hh in range(hp):
            s_ref[hh] = scores(hh, 0)

    def body(j, carry):
        for hh in range(hp):
            acc = attend(hh, j, False)
            s_ref[hh] = scores(hh, j + 1)
            acc_ref[hh] = acc
        return carry

    lax.fori_loop(0, i, body, 0)

    lv = lam_ref[...]
    lam = (jnp.exp(jnp.sum(lv[0:1] * lv[1:2], axis=-1, keepdims=True))
           - jnp.exp(jnp.sum(lv[2:3] * lv[3:4], axis=-1, keepdims=True)) + lambda_init)
    for hh in range(hp):
        acc = attend(hh, i, True)
        s_ref[hh] = scores_of(stacked(qn_ref, hh), hh, 0)
        m_ref[hh] = jnp.full((2 * t, LANES), MASK_VALUE, F32)
        acc_ref[hh] = jnp.zeros((2 * t, 2 * DIFF_V_DIM), F32)
        attn = acc[:, :DIFF_V_DIM] / acc[:, DIFF_V_DIM:]
        o = attn[:t] - lam * attn[t:]
        o = _rms_norm(o, gsub_ref[...]) * (1.0 - lambda_init)
        o_ref[0, :, hh * DIFF_V_DIM:(hh + 1) * DIFF_V_DIM] = o.astype(BF16)


def _diff_attention(qkv, lam_vecs, g_sub, lambda_init, *, t=512, hp=4):
    _, b, s, _ = qkv.shape
    hg = DIFF_HEADS // hp
    return pl.pallas_call(
        functools.partial(_diff_attn_kernel, t=t, hp=hp, lambda_init=lambda_init),
        out_shape=jax.ShapeDtypeStruct((b, s, D_MODEL), BF16),
        grid=(b, hg, s // t),
        in_specs=[
            pl.BlockSpec((4, DIFF_SUB_DIM), lambda bi, hi, i: (0, 0)),
            pl.BlockSpec((1, DIFF_V_DIM), lambda bi, hi, i: (0, 0)),
            pl.BlockSpec((hp, 1, t, DIFF_V_DIM), lambda bi, hi, i: (hi, bi, i, 0)),
            pl.BlockSpec((hp, 1, t, DIFF_V_DIM),
                         lambda bi, hi, i: (hi, bi, jnp.minimum(i + 1, s // t - 1), 0)),
            pl.BlockSpec((hp, 1, s, DIFF_V_DIM), lambda bi, hi, i: (hg + hi, bi, 0, 0)),
            pl.BlockSpec((hp, 1, s, DIFF_V_DIM), lambda bi, hi, i: (2 * hg + hi, bi, 0, 0)),
        ],
        out_specs=pl.BlockSpec((1, t, hp * DIFF_V_DIM), lambda bi, hi, i: (bi, i, hi)),
        scratch_shapes=[pltpu.VMEM((hp, 2 * t, t), F32), pltpu.VMEM((hp, 2 * t, LANES), F32),
                        pltpu.VMEM((hp, 2 * t, 2 * DIFF_V_DIM), F32)],
        compiler_params=_params("arbitrary", "arbitrary", "arbitrary"),
        name="diff_attention",
    )(lam_vecs, g_sub.reshape(1, DIFF_V_DIM), qkv, qkv, qkv, qkv)


def _swa_kernel(sink_ref, q_ref, kp_ref, kc_ref, vp_ref, vc_ref, o_ref, bias_ref, *, tq, pair):
    i = pl.program_id(1)
    kk = jnp.concatenate([kp_ref[0], kc_ref[0]], axis=0)
    vv = jnp.concatenate([vp_ref[0], vc_ref[0]], axis=0)
    ones = jnp.ones((2 * BLOCK, LANES), BF16)
    rows = 2 * pair * BLOCK
    lane_q = lax.broadcasted_iota(jnp.int32, (BLOCK, LANES), 1)
    qi = lax.broadcasted_iota(jnp.int32, (BLOCK, 2 * BLOCK), 0)
    kj = lax.broadcasted_iota(jnp.int32, (BLOCK, 2 * BLOCK), 1)
    in_window = (kj > qi) & (kj <= qi + WINDOW)
    bias_ref[0] = jnp.where(in_window, 0.0, MASK_VALUE)
    bias_ref[1] = jnp.where(in_window & (kj >= jnp.where(i > 0, 0, BLOCK)), 0.0, MASK_VALUE)
    for c in range(tq // BLOCK):
        blk = slice(c * BLOCK, (c + 1) * BLOCK)
        k = kk[c * BLOCK:(c + 2) * BLOCK]
        v_aug = jnp.concatenate([vv[c * BLOCK:(c + 2) * BLOCK], ones], axis=1)
        for jc in range(D_MODEL // LANES // pair):
            parts = []
            for j in range(jc * pair, (jc + 1) * pair):
                qj = q_ref[0, blk, j * LANES:(j + 1) * LANES]
                parts.append(jnp.where(lane_q < SWA_HEAD_DIM, qj, jnp.zeros_like(qj)))
                parts.append(jnp.where(lane_q >= SWA_HEAD_DIM, qj, jnp.zeros_like(qj)))
            q2 = jnp.concatenate(parts, axis=0)
            s = lax.dot_general(q2, k, (((1,), (1,)), ((), ())), preferred_element_type=F32)
            s = s + jnp.tile(bias_ref[1 if c == 0 else 0], (rows // BLOCK, 1))
            sink = sink_ref[jc * rows:(jc + 1) * rows, :]
            m = jnp.maximum(jnp.max(s, axis=-1, keepdims=True), sink)
            p = jnp.exp2(s - jnp.tile(m, (1, 2))).astype(BF16)
            pv = jnp.dot(p, v_aug, preferred_element_type=F32)
            o2 = pv[:, :LANES] / (pv[:, LANES:] + jnp.exp2(sink - m))
            for jj in range(pair):
                j = jc * pair + jj
                lo = o2[(2 * jj) * BLOCK:(2 * jj + 1) * BLOCK]
                hi = o2[(2 * jj + 1) * BLOCK:(2 * jj + 2) * BLOCK]
                o_ref[0, blk, j * LANES:(j + 1) * LANES] = (
                    jnp.where(lane_q < SWA_HEAD_DIM, lo, hi).astype(BF16))


def _swa_attention(q, kv, sink_rows, *, tq=512, pair=2):
    b, s, _ = q.shape
    per = tq // BLOCK
    cur = lambda bi, i: (bi, i, 0)
    prev = lambda bi, i: (bi, jnp.maximum(i * per - 1, 0), 0)
    cur_v = lambda bi, i: (bi, i, 1)
    prev_v = lambda bi, i: (bi, jnp.maximum(i * per - 1, 0), 1)
    rows = sink_rows.shape[0]
    return pl.pallas_call(
        functools.partial(_swa_kernel, tq=tq, pair=pair),
        out_shape=jax.ShapeDtypeStruct((b, s, D_MODEL), BF16),
        grid=(b, s // tq),
        in_specs=[
            pl.BlockSpec((rows, LANES), lambda bi, i: (0, 0)),
            pl.BlockSpec((1, tq, D_MODEL), cur),
            pl.BlockSpec((1, BLOCK, LANES), prev),
            pl.BlockSpec((1, tq, LANES), cur),
            pl.BlockSpec((1, BLOCK, LANES), prev_v),
            pl.BlockSpec((1, tq, LANES), cur_v),
        ],
        out_specs=pl.BlockSpec((1, tq, D_MODEL), cur),
        scratch_shapes=[pltpu.VMEM((2, BLOCK, 2 * BLOCK), F32)],
        compiler_params=_params("arbitrary", "arbitrary"),
        name="swa_attention",
    )(sink_rows, q, kv, kv, kv, kv)


def _paired_head_order():
    cols = []
    for j in range(SWA_GROUP):
        cols.extend(range(j * SWA_HEAD_DIM, (j + 1) * SWA_HEAD_DIM))
        cols.extend(range((SWA_GROUP + j) * SWA_HEAD_DIM, (SWA_GROUP + j + 1) * SWA_HEAD_DIM))
    return jnp.asarray(cols, dtype=jnp.int32)


def kernel(x, positions, ln_ffn1, ffn1_w_gate, ffn1_w_up, ffn1_w_down, ln_mix, ln_ffn2, ffn2_w_gate, ffn2_w_up, ffn2_w_down, a_w_qkv, a_w_o, a_lambda_q1, a_lambda_k1, a_lambda_q2, a_lambda_k2, a_subln, b_w_q, b_b_q, b_sinks, b_w_o, b_b_o, kv_norm, kv_w_k, kv_b_k, kv_w_v, kv_b_v, final_norm):
    batch, seq, _ = x.shape
    n = batch * seq
    tables = _rope_tables(positions)
    h = x.reshape(n, D_MODEL)
    perm = _paired_head_order()
    kv = None
    for layer in range(DEPTH):
        if layer < N_A_LAYERS:
            a = layer
            proj = (ln_mix[layer], a_w_qkv[a].astype(BF16), jnp.zeros((3 * D_MODEL,), F32), tables,
                    dict(rope_cols=2 * D_MODEL, scale_cols=D_MODEL,
                         scale=DIFF_SUB_DIM ** -0.5 * LOG2_E, head_major=True))
        else:
            b = layer - N_A_LAYERS
            proj = (ln_mix[layer], b_w_q[b][:, perm].astype(BF16), b_b_q[b][perm], tables,
                    dict(rope_cols=D_MODEL, scale_cols=D_MODEL,
                         scale=SWA_HEAD_DIM ** -0.5 * LOG2_E, head_major=False))
        h, mixer_in = _ffn(h, ln_ffn1[layer], ffn1_w_gate[layer].astype(BF16),
                           ffn1_w_up[layer].astype(BF16), ffn1_w_down[layer].astype(BF16), proj=proj)
        if layer < N_A_LAYERS:
            lambda_init = 0.8 - 0.6 * math.exp(-0.3 * layer)
            lam_vecs = jnp.stack([a_lambda_q1[a], a_lambda_k1[a], a_lambda_q2[a], a_lambda_k2[a]])
            o = _diff_attention(mixer_in.reshape(3 * DIFF_HEADS, batch, seq, DIFF_V_DIM), lam_vecs,
                                a_subln[a], lambda_init)
            attn = (o.reshape(n, D_MODEL), a_w_o[a].astype(BF16), jnp.zeros((D_MODEL,), F32))
        else:
            sink_pairs = jnp.stack([b_sinks[b][:SWA_GROUP], b_sinks[b][SWA_GROUP:]], axis=1)
            sink_rows = jnp.broadcast_to(
                jnp.repeat(sink_pairs.reshape(-1) * LOG2_E, BLOCK)[:, None], (2 * D_MODEL, LANES))
            o = _swa_attention(mixer_in.reshape(batch, seq, D_MODEL), kv, sink_rows)
            attn = (o.reshape(n, D_MODEL), b_w_o[b][perm, :].astype(BF16), b_b_o[b])
        proj = None
        if layer == N_A_LAYERS - 1:
            proj = (kv_norm, jnp.concatenate([kv_w_k, kv_w_v], axis=1).astype(BF16),
                    jnp.concatenate([kv_b_k, kv_b_v]), tables,
                    dict(rope_cols=SWA_KV_WIDTH, scale_cols=0, scale=1.0, head_major=False))
        res = _ffn(h, ln_ffn2[layer], ffn2_w_gate[layer].astype(BF16), ffn2_w_up[layer].astype(BF16),
                   ffn2_w_down[layer].astype(BF16), attn=attn, proj=proj,
                   final_g=final_norm if layer == DEPTH - 1 else None)
        if proj is None:
            h = res
        else:
            h, kv = res[0], res[1].reshape(batch, seq, 2 * SWA_KV_WIDTH)
    return h.reshape(batch, seq, D_MODEL)
```

```python
import functools
import math

import jax
import jax.numpy as jnp
from jax import lax
from jax.experimental import pallas as pl
from jax.experimental.pallas import tpu as pltpu

D_MODEL = 1024
D_FF = 2816
DEPTH = 4
N_A_LAYERS = 2
DIFF_HEADS = 8
DIFF_SUB_DIM = 64
DIFF_V_DIM = 128
SWA_Q_HEADS = 16
SWA_KV_HEADS = 2
SWA_HEAD_DIM = 64
SWA_GROUP = SWA_Q_HEADS // SWA_KV_HEADS
SWA_KV_WIDTH = SWA_KV_HEADS * SWA_HEAD_DIM
WINDOW = 128
BLOCK = 128
ROPE_THETA = 500000.0
ROT_DIM = 16
ROT_HALF = ROT_DIM // 2
NORM_EPS = 1e-5

LANES = 128
VMEM_LIMIT = 56 * 1024 * 1024
VMEM_COMPILER_RESERVE = 6 * 1024 * 1024
MASK_VALUE = -1e30
LOG2_E = math.log2(math.e)

F32 = jnp.float32
BF16 = jnp.bfloat16


def _params(*sem):
    return pltpu.CompilerParams(dimension_semantics=sem, vmem_limit_bytes=VMEM_LIMIT)


def _resident(shape):
    return pl.BlockSpec(shape, lambda *_: (0,) * len(shape), pipeline_mode=pl.Buffered(1))


def _rms_norm(x, g):
    ms = jnp.mean(x * x, axis=-1, keepdims=True)
    return x * lax.rsqrt(ms + NORM_EPS) * g


def _rope_table_kernel(pos_ref, invf_ref, c_ref, s1_ref, s2_ref):
    ang = pos_ref[...] * invf_ref[...]
    cos = jnp.cos(ang)
    sin = jnp.sin(ang)
    lane = lax.broadcasted_iota(jnp.int32, ang.shape, 1) % SWA_HEAD_DIM
    first = lane < ROT_HALF
    second = (lane >= ROT_HALF) & (lane < ROT_DIM)
    c_ref[...] = jnp.where(first | second, cos, 1.0)
    s1_ref[...] = jnp.where(first, -sin, 0.0)
    s2_ref[...] = jnp.where(second, sin, 0.0)


def _rope_tables(positions, tm=1024):
    n = positions.size
    pos = jnp.broadcast_to(positions.reshape(n, 1).astype(F32), (n, LANES))
    inv_freq = ROPE_THETA ** (-jnp.arange(0, ROT_DIM, 2, dtype=F32) / ROT_DIM)
    invf = jnp.tile(inv_freq, LANES // ROT_HALF).reshape(1, LANES)
    spec = pl.BlockSpec((tm, LANES), lambda i: (i, 0))
    out = jax.ShapeDtypeStruct((n, LANES), F32)
    return pl.pallas_call(
        _rope_table_kernel,
        out_shape=(out, out, out),
        grid=(n // tm,),
        in_specs=[spec, pl.BlockSpec((1, LANES), lambda i: (0, 0))],
        out_specs=(spec, spec, spec),
        compiler_params=_params("arbitrary"),
        name="rope_tables",
    )(pos, invf)


def _apply_rope(x, c, s1, s2):
    return (x * c + pltpu.roll(x, LANES - ROT_HALF, 1) * s1 + pltpu.roll(x, ROT_HALF, 1) * s2)


def _project(xn_ref, w_ref, b_ref, tables, o_ref, *, rope_cols, scale_cols, scale, head_major):
    n_out = w_ref.shape[1]
    chunk = min(n_out, 512)
    c, s1, s2 = tables
    for c0 in range(0, n_out, chunk):
        cols = slice(c0, c0 + chunk)
        y = jnp.dot(xn_ref[...], w_ref[:, cols], preferred_element_type=F32) + b_ref[:, cols]
        for j in range(0, chunk, LANES):
            yj = y[:, j:j + LANES]
            if c0 + j < rope_cols:
                yj = _apply_rope(yj, c, s1, s2)
            if c0 + j < scale_cols:
                yj = yj * scale
            if head_major:
                o_ref[(c0 + j) // LANES] = yj.astype(BF16)
            else:
                o_ref[:, c0 + j:c0 + j + LANES] = yj.astype(BF16)


def _ffn_kernel(*refs, tf, has_attn, has_final, proj):
    h_ref, g_ref, wg_ref, wu_ref, wd_ref = refs[:5]
    rest = list(refs[5:])
    if has_attn:
        a_ref, wo_ref, bo_ref = rest[:3]
        rest = rest[3:]
    if has_final:
        gf_ref = rest.pop(0)
    if proj is not None:
        gp_ref, wp_ref, bp_ref, c_ref, s1_ref, s2_ref = rest[:6]
        rest = rest[6:]
        p_ref = rest.pop(1)
    o_ref, xn_ref, acc_ref = rest[:3]
    h = h_ref[...]
    if has_attn:
        res_ref = rest[3]
        h = h + bo_ref[...] + jnp.dot(a_ref[...], wo_ref[...], preferred_element_type=F32)
        res_ref[...] = h
    else:
        res_ref = h_ref
    xn_ref[...] = _rms_norm(h, g_ref[...]).astype(BF16)
    for c in range(D_FF // tf):
        cols = slice(c * tf, (c + 1) * tf)
        xn = xn_ref[...]
        gate = jnp.dot(xn, wg_ref[:, cols], preferred_element_type=F32)
        up = jnp.dot(xn, wu_ref[:, cols], preferred_element_type=F32)
        act = (gate / (1.0 + jnp.exp(-gate)) * up).astype(BF16)
        down = jnp.dot(act, wd_ref[cols, :], preferred_element_type=F32)
        if c == 0:
            acc_ref[...] = down
        else:
            acc_ref[...] += down
    out = res_ref[...] + 0.5 * acc_ref[...]
    if has_final:
        out = _rms_norm(out, gf_ref[...])
    o_ref[...] = out
    if proj is not None:
        xn_ref[...] = _rms_norm(out, gp_ref[...]).astype(BF16)
        _project(xn_ref, wp_ref, bp_ref, (c_ref[...], s1_ref[...], s2_ref[...]), p_ref, **proj)


def _ffn_row_tile(has_attn, proj_cols):
    weights = 2 * (3 * D_MODEL * D_FF + (D_MODEL * D_MODEL if has_attn else 0)
                   + D_MODEL * proj_cols)
    per_row = 2 * 4 * D_MODEL * 2 + 2 * D_MODEL + 4 * D_MODEL
    if has_attn:
        per_row += 2 * 2 * D_MODEL + 4 * D_MODEL
    if proj_cols:
        per_row += 2 * 2 * proj_cols + 2 * 3 * 4 * LANES
    for tm in (1024, 512):
        if weights + tm * per_row <= VMEM_LIMIT - VMEM_COMPILER_RESERVE:
            return tm
    raise ValueError("FFN blocks do not fit in VMEM")


def _ffn(h, g, wg, wu, wd, *, attn=None, final_g=None, proj=None, tf=256):
    n = h.shape[0]
    tm = _ffn_row_tile(attn is not None, 0 if proj is None else proj[1].shape[1])
    row = pl.BlockSpec((tm, D_MODEL), lambda i: (i, 0))
    vec = _resident((1, D_MODEL))
    args = [h, g.reshape(1, D_MODEL), wg, wu, wd]
    specs = [row, vec, _resident((D_MODEL, D_FF)), _resident((D_MODEL, D_FF)),
             _resident((D_FF, D_MODEL))]
    scratch = [pltpu.VMEM((tm, D_MODEL), BF16), pltpu.VMEM((tm, D_MODEL), F32)]
    out_shape = jax.ShapeDtypeStruct((n, D_MODEL), F32)
    out_specs = row
    if attn is not None:
        a, wo, bo = attn
        args += [a, wo, bo.reshape(1, D_MODEL)]
        specs += [row, _resident((D_MODEL, D_MODEL)), vec]
        scratch.append(pltpu.VMEM((tm, D_MODEL), F32))
    if final_g is not None:
        args.append(final_g.reshape(1, D_MODEL))
        specs.append(vec)
    options = None
    if proj is not None:
        gp, wp, bp, tables, options = proj
        n_out = wp.shape[1]
        assert (options["rope_cols"] % LANES == 0 and options["scale_cols"] % LANES == 0
                and n_out % min(n_out, 512) == 0)
        tab = pl.BlockSpec((tm, LANES), lambda i: (i, 0))
        args += [gp.reshape(1, D_MODEL), wp, bp.reshape(1, n_out), *tables]
        specs += [vec, _resident((D_MODEL, n_out)), _resident((1, n_out)), tab, tab, tab]
        if options["head_major"]:
            p_shape = jax.ShapeDtypeStruct((n_out // LANES, n, LANES), BF16)
            p_spec = pl.BlockSpec((n_out // LANES, tm, LANES), lambda i: (0, i, 0))
        else:
            p_shape = jax.ShapeDtypeStruct((n, n_out), BF16)
            p_spec = pl.BlockSpec((tm, n_out), lambda i: (i, 0))
        out_shape = (out_shape, p_shape)
        out_specs = (row, p_spec)
    return pl.pallas_call(
        functools.partial(_ffn_kernel, tf=tf, has_attn=attn is not None,
                          has_final=final_g is not None, proj=options),
        out_shape=out_shape,
        grid=(n // tm,),
        in_specs=specs,
        out_specs=out_specs,
        scratch_shapes=scratch,
        compiler_params=_params("arbitrary"),
        name="ffn",
    )(*args)


def _diff_attn_kernel(lam_ref, gsub_ref, q_ref, qn_ref, k_ref, v_ref, o_ref, s_ref, m_ref, acc_ref,
                      *, t, hp, lambda_init):
    i = pl.program_id(2)
    lane = lax.broadcasted_iota(jnp.int32, (t, LANES), 1)
    ones = jnp.ones((t, LANES), BF16)

    def stacked(ref, hh):
        q = ref[hh, 0]
        zero = jnp.zeros_like(q)
        return jnp.concatenate([jnp.where(lane < DIFF_SUB_DIM, q, zero),
                                jnp.where(lane >= DIFF_SUB_DIM, q, zero)], axis=0)

    q2 = [stacked(q_ref, hh) for hh in range(hp)]

    def scores_of(q_stacked, hh, j):
        k = k_ref[hh, 0, pl.ds(j * t, t), :]
        return lax.dot_general(q_stacked, k, (((1,), (1,)), ((), ())), preferred_element_type=F32)

    def scores(hh, j):
        return scores_of(q2[hh], hh, j)

    def online_softmax_update(s, m_prev, acc_prev, v_aug):
        m_next = jnp.maximum(m_prev, jnp.max(s, axis=-1, keepdims=True))
        alpha = jnp.tile(jnp.exp2(m_prev - m_next), (1, 2))
        p = jnp.exp2(s - jnp.tile(m_next, (1, s.shape[1] // LANES))).astype(BF16)
        return m_next, alpha * acc_prev + jnp.dot(p, v_aug, preferred_element_type=F32)

    def attend(hh, j):
        v_aug = jnp.concatenate([v_ref[hh, 0, pl.ds(j * t, t), :], ones], axis=1)
        m_next, acc = online_softmax_update(s_ref[hh], m_ref[hh], acc_ref[hh], v_aug)
        m_ref[hh] = m_next
        return acc

    def attend_diagonal(hh):
        half = t // 2
        keep = jnp.tile(lax.broadcasted_iota(jnp.int32, (half, half), 1)
                        <= lax.broadcasted_iota(jnp.int32, (half, half), 0), (2, 1))
        v_aug = jnp.concatenate([v_ref[hh, 0, pl.ds(i * t, t), :], ones], axis=1)
        parts = []
        for first_half in (True, False):
            r0 = 0 if first_half else half
            rows = (slice(r0, r0 + half), slice(t + r0, t + r0 + half))
            n_keys = half if first_half else t
            s = jnp.concatenate([s_ref[hh, r, 0:n_keys] for r in rows], axis=0)
            causal = jnp.where(keep, s[:, n_keys - half:], MASK_VALUE)
            s = causal if first_half else jnp.concatenate([s[:, :half], causal], axis=1)
            m_prev = jnp.concatenate([m_ref[hh, r] for r in rows], axis=0)
            acc_prev = jnp.concatenate([acc_ref[hh, r] for r in rows], axis=0)
            parts.append(online_softmax_update(s, m_prev, acc_prev, v_aug[0:n_keys])[1])
        top, bottom = parts
        return jnp.concatenate([top[:half], bottom[:half], top[half:], bottom[half:]], axis=0)

    @pl.when(i == 0)
    def _():
        m_ref[...] = jnp.full(m_ref.shape, MASK_VALUE, F32)
        acc_ref[...] = jnp.zeros(acc_ref.shape, F32)
        for hh in range(hp):
            s_ref[hh] = scores(hh, 0)

    def body(j, carry):
        for hh in range(hp):
            acc = attend(hh, j)
            s_ref[hh] = scores(hh, j + 1)
            acc_ref[hh] = acc
        return carry

    lax.fori_loop(0, i, body, 0)

    lv = lam_ref[...]
    lam = (jnp.exp(jnp.sum(lv[0:1] * lv[1:2], axis=-1, keepdims=True))
           - jnp.exp(jnp.sum(lv[2:3] * lv[3:4], axis=-1, keepdims=True)) + lambda_init)
    for hh in range(hp):
        acc = attend_diagonal(hh)
        s_ref[hh] = scores_of(stacked(qn_ref, hh), hh, 0)
        m_ref[hh] = jnp.full((2 * t, LANES), MASK_VALUE, F32)
        acc_ref[hh] = jnp.zeros((2 * t, 2 * DIFF_V_DIM), F32)
        attn = acc[:, :DIFF_V_DIM] / acc[:, DIFF_V_DIM:]
        o = attn[:t] - lam * attn[t:]
        o = _rms_norm(o, gsub_ref[...]) * (1.0 - lambda_init)
        o_ref[0, :, hh * DIFF_V_DIM:(hh + 1) * DIFF_V_DIM] = o.astype(BF16)


def _diff_attention(qkv, lam_vecs, g_sub, lambda_init, *, t=512, hp=4):
    _, b, s, _ = qkv.shape
    hg = DIFF_HEADS // hp
    return pl.pallas_call(
        functools.partial(_diff_attn_kernel, t=t, hp=hp, lambda_init=lambda_init),
        out_shape=jax.ShapeDtypeStruct((b, s, D_MODEL), BF16),
        grid=(b, hg, s // t),
        in_specs=[
            pl.BlockSpec((4, DIFF_SUB_DIM), lambda bi, hi, i: (0, 0)),
            pl.BlockSpec((1, DIFF_V_DIM), lambda bi, hi, i: (0, 0)),
            pl.BlockSpec((hp, 1, t, DIFF_V_DIM), lambda bi, hi, i: (hi, bi, i, 0)),
            pl.BlockSpec((hp, 1, t, DIFF_V_DIM),
                         lambda bi, hi, i: (hi, bi, jnp.minimum(i + 1, s // t - 1), 0)),
            pl.BlockSpec((hp, 1, s, DIFF_V_DIM), lambda bi, hi, i: (hg + hi, bi, 0, 0)),
            pl.BlockSpec((hp, 1, s, DIFF_V_DIM), lambda bi, hi, i: (2 * hg + hi, bi, 0, 0)),
        ],
        out_specs=pl.BlockSpec((1, t, hp * DIFF_V_DIM), lambda bi, hi, i: (bi, i, hi)),
        scratch_shapes=[pltpu.VMEM((hp, 2 * t, t), F32), pltpu.VMEM((hp, 2 * t, LANES), F32),
                        pltpu.VMEM((hp, 2 * t, 2 * DIFF_V_DIM), F32)],
        compiler_params=_params("arbitrary", "arbitrary", "arbitrary"),
        name="diff_attention",
    )(lam_vecs, g_sub.reshape(1, DIFF_V_DIM), qkv, qkv, qkv, qkv)


def _swa_kernel(sink_ref, q_ref, kp_ref, kc_ref, vp_ref, vc_ref, o_ref, bias_ref, *, tq, pair):
    i = pl.program_id(1)
    kk = jnp.concatenate([kp_ref[0], kc_ref[0]], axis=0)
    vv = jnp.concatenate([vp_ref[0], vc_ref[0]], axis=0)
    ones = jnp.ones((2 * BLOCK, LANES), BF16)
    rows = 2 * pair * BLOCK
    lane_q = lax.broadcasted_iota(jnp.int32, (BLOCK, LANES), 1)
    qi = lax.broadcasted_iota(jnp.int32, (BLOCK, 2 * BLOCK), 0)
    kj = lax.broadcasted_iota(jnp.int32, (BLOCK, 2 * BLOCK), 1)
    in_window = (kj > qi) & (kj <= qi + WINDOW)
    bias_ref[0] = jnp.where(in_window, 0.0, MASK_VALUE)
    bias_ref[1] = jnp.where(in_window & (kj >= jnp.where(i > 0, 0, BLOCK)), 0.0, MASK_VALUE)
    for c in range(tq // BLOCK):
        blk = slice(c * BLOCK, (c + 1) * BLOCK)
        k = kk[c * BLOCK:(c + 2) * BLOCK]
        v_aug = jnp.concatenate([vv[c * BLOCK:(c + 2) * BLOCK], ones], axis=1)
        for jc in range(D_MODEL // LANES // pair):
            parts = []
            for j in range(jc * pair, (jc + 1) * pair):
                qj = q_ref[0, blk, j * LANES:(j + 1) * LANES]
                parts.append(jnp.where(lane_q < SWA_HEAD_DIM, qj, jnp.zeros_like(qj)))
                parts.append(jnp.where(lane_q >= SWA_HEAD_DIM, qj, jnp.zeros_like(qj)))
            q2 = jnp.concatenate(parts, axis=0)
            s = lax.dot_general(q2, k, (((1,), (1,)), ((), ())), preferred_element_type=F32)
            s = s + jnp.tile(bias_ref[1 if c == 0 else 0], (rows // BLOCK, 1))
            sink = sink_ref[jc * rows:(jc + 1) * rows, :]
            m = jnp.maximum(jnp.max(s, axis=-1, keepdims=True), sink)
            p = jnp.exp2(s - jnp.tile(m, (1, 2))).astype(BF16)
            pv = jnp.dot(p, v_aug, preferred_element_type=F32)
            o2 = pv[:, :LANES] / (pv[:, LANES:] + jnp.exp2(sink - m))
            for jj in range(pair):
                j = jc * pair + jj
                lo = o2[(2 * jj) * BLOCK:(2 * jj + 1) * BLOCK]
                hi = o2[(2 * jj + 1) * BLOCK:(2 * jj + 2) * BLOCK]
                o_ref[0, blk, j * LANES:(j + 1) * LANES] = (
                    jnp.where(lane_q < SWA_HEAD_DIM, lo, hi).astype(BF16))


def _swa_attention(q, kv, sink_rows, *, tq=512, pair=2):
    b, s, _ = q.shape
    per = tq // BLOCK
    cur = lambda bi, i: (bi, i, 0)
    prev = lambda bi, i: (bi, jnp.maximum(i * per - 1, 0), 0)
    cur_v = lambda bi, i: (bi, i, 1)
    prev_v = lambda bi, i: (bi, jnp.maximum(i * per - 1, 0), 1)
    rows = sink_rows.shape[0]
    return pl.pallas_call(
        functools.partial(_swa_kernel, tq=tq, pair=pair),
        out_shape=jax.ShapeDtypeStruct((b, s, D_MODEL), BF16),
        grid=(b, s // tq),
        in_specs=[
            pl.BlockSpec((rows, LANES), lambda bi, i: (0, 0)),
            pl.BlockSpec((1, tq, D_MODEL), cur),
            pl.BlockSpec((1, BLOCK, LANES), prev),
            pl.BlockSpec((1, tq, LANES), cur),
            pl.BlockSpec((1, BLOCK, LANES), prev_v),
            pl.BlockSpec((1, tq, LANES), cur_v),
        ],
        out_specs=pl.BlockSpec((1, tq, D_MODEL), cur),
        scratch_shapes=[pltpu.VMEM((2, BLOCK, 2 * BLOCK), F32)],
        compiler_params=_params("arbitrary", "arbitrary"),
        name="swa_attention",
    )(sink_rows, q, kv, kv, kv, kv)


def _paired_head_order():
    cols = []
    for j in range(SWA_GROUP):
        cols.extend(range(j * SWA_HEAD_DIM, (j + 1) * SWA_HEAD_DIM))
        cols.extend(range((SWA_GROUP + j) * SWA_HEAD_DIM, (SWA_GROUP + j + 1) * SWA_HEAD_DIM))
    return jnp.asarray(cols, dtype=jnp.int32)


def kernel(x, positions, ln_ffn1, ffn1_w_gate, ffn1_w_up, ffn1_w_down, ln_mix, ln_ffn2, ffn2_w_gate, ffn2_w_up, ffn2_w_down, a_w_qkv, a_w_o, a_lambda_q1, a_lambda_k1, a_lambda_q2, a_lambda_k2, a_subln, b_w_q, b_b_q, b_sinks, b_w_o, b_b_o, kv_norm, kv_w_k, kv_b_k, kv_w_v, kv_b_v, final_norm):
    batch, seq, _ = x.shape
    n = batch * seq
    tables = _rope_tables(positions)
    h = x.reshape(n, D_MODEL)
    perm = _paired_head_order()
    kv = None
    for layer in range(DEPTH):
        if layer < N_A_LAYERS:
            a = layer
            proj = (ln_mix[layer], a_w_qkv[a].astype(BF16), jnp.zeros((3 * D_MODEL,), F32), tables,
                    dict(rope_cols=2 * D_MODEL, scale_cols=D_MODEL,
                         scale=DIFF_SUB_DIM ** -0.5 * LOG2_E, head_major=True))
        else:
            b = layer - N_A_LAYERS
            proj = (ln_mix[layer], b_w_q[b][:, perm].astype(BF16), b_b_q[b][perm], tables,
                    dict(rope_cols=D_MODEL, scale_cols=D_MODEL,
                         scale=SWA_HEAD_DIM ** -0.5 * LOG2_E, head_major=False))
        h, mixer_in = _ffn(h, ln_ffn1[layer], ffn1_w_gate[layer].astype(BF16),
                           ffn1_w_up[layer].astype(BF16), ffn1_w_down[layer].astype(BF16), proj=proj)
        if layer < N_A_LAYERS:
            lambda_init = 0.8 - 0.6 * math.exp(-0.3 * layer)
            lam_vecs = jnp.stack([a_lambda_q1[a], a_lambda_k1[a], a_lambda_q2[a], a_lambda_k2[a]])
            o = _diff_attention(mixer_in.reshape(3 * DIFF_HEADS, batch, seq, DIFF_V_DIM), lam_vecs,
                                a_subln[a], lambda_init)
            attn = (o.reshape(n, D_MODEL), a_w_o[a].astype(BF16), jnp.zeros((D_MODEL,), F32))
        else:
            sink_pairs = jnp.stack([b_sinks[b][:SWA_GROUP], b_sinks[b][SWA_GROUP:]], axis=1)
            sink_rows = jnp.broadcast_to(
                jnp.repeat(sink_pairs.reshape(-1) * LOG2_E, BLOCK)[:, None], (2 * D_MODEL, LANES))
            o = _swa_attention(mixer_in.reshape(batch, seq, D_MODEL), kv, sink_rows)
            attn = (o.reshape(n, D_MODEL), b_w_o[b][perm, :].astype(BF16), b_b_o[b])
        proj = None
        if layer == N_A_LAYERS - 1:
            proj = (kv_norm, jnp.concatenate([kv_w_k, kv_w_v], axis=1).astype(BF16),
                    jnp.concatenate([kv_b_k, kv_b_v]), tables,
                    dict(rope_cols=SWA_KV_WIDTH, scale_cols=0, scale=1.0, head_major=False))
        res = _ffn(h, ln_ffn2[layer], ffn2_w_gate[layer].astype(BF16), ffn2_w_up[layer].astype(BF16),
                   ffn2_w_down[layer].astype(BF16), attn=attn, proj=proj,
                   final_g=final_norm if layer == DEPTH - 1 else None)
        if proj is None:
            h = res
        else:
            h, kv = res[0], res[1].reshape(batch, seq, 2 * SWA_KV_WIDTH)
    return h.reshape(batch, seq, D_MODEL)
```

```python
import functools
import math

import jax
import jax.numpy as jnp
from jax import lax
from jax.experimental import pallas as pl
from jax.experimental.pallas import tpu as pltpu

D_MODEL = 1024
D_FF = 2816
DEPTH = 4
N_A_LAYERS = 2
DIFF_HEADS = 8
DIFF_SUB_DIM = 64
DIFF_V_DIM = 128
SWA_Q_HEADS = 16
SWA_KV_HEADS = 2
SWA_HEAD_DIM = 64
SWA_GROUP = SWA_Q_HEADS // SWA_KV_HEADS
SWA_KV_WIDTH = SWA_KV_HEADS * SWA_HEAD_DIM
WINDOW = 128
BLOCK = 128
ROPE_THETA = 500000.0
ROT_DIM = 16
ROT_HALF = ROT_DIM // 2
NORM_EPS = 1e-5

LANES = 128
VMEM_LIMIT = 56 * 1024 * 1024
VMEM_COMPILER_RESERVE = 6 * 1024 * 1024
MASK_VALUE = -1e30
LOG2_E = math.log2(math.e)

F32 = jnp.float32
BF16 = jnp.bfloat16


def _params(*sem):
    return pltpu.CompilerParams(dimension_semantics=sem, vmem_limit_bytes=VMEM_LIMIT)


def _resident(shape):
    return pl.BlockSpec(shape, lambda *_: (0,) * len(shape), pipeline_mode=pl.Buffered(1))


def _rms_norm(x, g):
    ms = jnp.mean(x * x, axis=-1, keepdims=True)
    return x * lax.rsqrt(ms + NORM_EPS) * g


def _rope_table_kernel(pos_ref, invf_ref, c_ref, s1_ref, s2_ref):
    ang = pos_ref[...] * invf_ref[...]
    cos = jnp.cos(ang)
    sin = jnp.sin(ang)
    lane = lax.broadcasted_iota(jnp.int32, ang.shape, 1) % SWA_HEAD_DIM
    first = lane < ROT_HALF
    second = (lane >= ROT_HALF) & (lane < ROT_DIM)
    c_ref[...] = jnp.where(first | second, cos, 1.0)
    s1_ref[...] = jnp.where(first, -sin, 0.0)
    s2_ref[...] = jnp.where(second, sin, 0.0)


def _rope_tables(positions, tm=1024):
    n = positions.size
    pos = jnp.broadcast_to(positions.reshape(n, 1).astype(F32), (n, LANES))
    inv_freq = ROPE_THETA ** (-jnp.arange(0, ROT_DIM, 2, dtype=F32) / ROT_DIM)
    invf = jnp.tile(inv_freq, LANES // ROT_HALF).reshape(1, LANES)
    spec = pl.BlockSpec((tm, LANES), lambda i: (i, 0))
    out = jax.ShapeDtypeStruct((n, LANES), F32)
    return pl.pallas_call(
        _rope_table_kernel,
        out_shape=(out, out, out),
        grid=(n // tm,),
        in_specs=[spec, pl.BlockSpec((1, LANES), lambda i: (0, 0))],
        out_specs=(spec, spec, spec),
        compiler_params=_params("arbitrary"),
        name="rope_tables",
    )(pos, invf)


def _apply_rope(x, c, s1, s2):
    return (x * c + pltpu.roll(x, LANES - ROT_HALF, 1) * s1 + pltpu.roll(x, ROT_HALF, 1) * s2)


def _project(xn_ref, w_ref, b_ref, tables, o_ref, *, rope_cols, scale_cols, scale, head_major):
    n_out = w_ref.shape[1]
    chunk = min(n_out, 512)
    c, s1, s2 = tables
    for c0 in range(0, n_out, chunk):
        cols = slice(c0, c0 + chunk)
        y = jnp.dot(xn_ref[...], w_ref[:, cols], preferred_element_type=F32) + b_ref[:, cols]
        for j in range(0, chunk, LANES):
            yj = y[:, j:j + LANES]
            if c0 + j < rope_cols:
                yj = _apply_rope(yj, c, s1, s2)
            if c0 + j < scale_cols:
                yj = yj * scale
            if head_major:
                o_ref[(c0 + j) // LANES] = yj.astype(BF16)
            else:
                o_ref[:, c0 + j:c0 + j + LANES] = yj.astype(BF16)


def _ffn_kernel(*refs, tf, has_attn, has_final, proj):
    h_ref, g_ref, wg_ref, wu_ref, wd_ref = refs[:5]
    rest = list(refs[5:])
    if has_attn:
        a_ref, wo_ref, bo_ref = rest[:3]
        rest = rest[3:]
    if has_final:
        gf_ref = rest.pop(0)
    if proj is not None:
        gp_ref, wp_ref, bp_ref, c_ref, s1_ref, s2_ref = rest[:6]
        rest = rest[6:]
        p_ref = rest.pop(1)
    o_ref, xn_ref = rest[:2]
    h = h_ref[...]
    if has_attn:
        res_ref = rest[2]
        h = h + bo_ref[...] + jnp.dot(a_ref[...], wo_ref[...], preferred_element_type=F32)
        res_ref[...] = h
    else:
        res_ref = h_ref
    xn_ref[...] = _rms_norm(h, g_ref[...]).astype(BF16)
    for c in range(D_FF // tf):
        cols = slice(c * tf, (c + 1) * tf)
        xn = xn_ref[...]
        gate = jnp.dot(xn, wg_ref[:, cols], preferred_element_type=F32)
        up = jnp.dot(xn, wu_ref[:, cols], preferred_element_type=F32)
        act = (gate / (1.0 + jnp.exp(-gate)) * up).astype(BF16)
        down = jnp.dot(act, wd_ref[cols, :], preferred_element_type=F32)
        acc = down if c == 0 else acc + down
    out = res_ref[...] + 0.5 * acc
    if has_final:
        out = _rms_norm(out, gf_ref[...])
    o_ref[...] = out
    if proj is not None:
        xn_ref[...] = _rms_norm(out, gp_ref[...]).astype(BF16)
        _project(xn_ref, wp_ref, bp_ref, (c_ref[...], s1_ref[...], s2_ref[...]), p_ref, **proj)


def _ffn_row_tile(has_attn, proj_cols):
    weights = 2 * (3 * D_MODEL * D_FF + (D_MODEL * D_MODEL if has_attn else 0)
                   + D_MODEL * proj_cols)
    per_row = 2 * 4 * D_MODEL * 2 + 2 * D_MODEL + 4 * D_MODEL
    if has_attn:
        per_row += 2 * 2 * D_MODEL + 4 * D_MODEL
    if proj_cols:
        per_row += 2 * 2 * proj_cols + 2 * 3 * 4 * LANES
    for tm in (1024, 512):
        if weights + tm * per_row <= VMEM_LIMIT - VMEM_COMPILER_RESERVE:
            return tm
    raise ValueError("FFN blocks do not fit in VMEM")


def _ffn(h, g, wg, wu, wd, *, attn=None, final_g=None, proj=None, tf=256):
    n = h.shape[0]
    tm = _ffn_row_tile(attn is not None, 0 if proj is None else proj[1].shape[1])
    row = pl.BlockSpec((tm, D_MODEL), lambda i: (i, 0))
    vec = _resident((1, D_MODEL))
    args = [h, g.reshape(1, D_MODEL), wg, wu, wd]
    specs = [row, vec, _resident((D_MODEL, D_FF)), _resident((D_MODEL, D_FF)),
             _resident((D_FF, D_MODEL))]
    scratch = [pltpu.VMEM((tm, D_MODEL), BF16)]
    out_shape = jax.ShapeDtypeStruct((n, D_MODEL), F32)
    out_specs = row
    if attn is not None:
        a, wo, bo = attn
        args += [a, wo, bo.reshape(1, D_MODEL)]
        specs += [row, _resident((D_MODEL, D_MODEL)), vec]
        scratch.append(pltpu.VMEM((tm, D_MODEL), F32))
    if final_g is not None:
        args.append(final_g.reshape(1, D_MODEL))
        specs.append(vec)
    options = None
    if proj is not None:
        gp, wp, bp, tables, options = proj
        n_out = wp.shape[1]
        assert (options["rope_cols"] % LANES == 0 and options["scale_cols"] % LANES == 0
                and n_out % min(n_out, 512) == 0)
        tab = pl.BlockSpec((tm, LANES), lambda i: (i, 0))
        args += [gp.reshape(1, D_MODEL), wp, bp.reshape(1, n_out), *tables]
        specs += [vec, _resident((D_MODEL, n_out)), _resident((1, n_out)), tab, tab, tab]
        if options["head_major"]:
            p_shape = jax.ShapeDtypeStruct((n_out // LANES, n, LANES), BF16)
            p_spec = pl.BlockSpec((n_out // LANES, tm, LANES), lambda i: (0, i, 0))
        else:
            p_shape = jax.ShapeDtypeStruct((n, n_out), BF16)
            p_spec = pl.BlockSpec((tm, n_out), lambda i: (i, 0))
        out_shape = (out_shape, p_shape)
        out_specs = (row, p_spec)
    return pl.pallas_call(
        functools.partial(_ffn_kernel, tf=tf, has_attn=attn is not None,
                          has_final=final_g is not None, proj=options),
        out_shape=out_shape,
        grid=(n // tm,),
        in_specs=specs,
        out_specs=out_specs,
        scratch_shapes=scratch,
        compiler_params=_params("arbitrary"),
        name="ffn",
    )(*args)


def _diff_attn_kernel(lam_ref, gsub_ref, q_ref, qn_ref, k_ref, v_ref, o_ref, s_ref, m_ref, acc_ref,
                      *, t, hp, lambda_init):
    i = pl.program_id(2)
    lane = lax.broadcasted_iota(jnp.int32, (t, LANES), 1)
    ones = jnp.ones((t, LANES), BF16)

    def stacked(ref, hh):
        q = ref[hh, 0]
        zero = jnp.zeros_like(q)
        return jnp.concatenate([jnp.where(lane < DIFF_SUB_DIM, q, zero),
                                jnp.where(lane >= DIFF_SUB_DIM, q, zero)], axis=0)

    q2 = [stacked(q_ref, hh) for hh in range(hp)]

    def scores_of(q_stacked, hh, j):
        k = k_ref[hh, 0, pl.ds(j * t, t), :]
        return lax.dot_general(q_stacked, k, (((1,), (1,)), ((), ())), preferred_element_type=F32)

    def scores(hh, j):
        return scores_of(q2[hh], hh, j)

    def attend(hh, j, masked):
        s = s_ref[hh]
        if masked:
            row = lax.broadcasted_iota(jnp.int32, s.shape, 0) % t
            col = lax.broadcasted_iota(jnp.int32, s.shape, 1)
            s = jnp.where(col <= row, s, MASK_VALUE)
        m_prev = m_ref[hh]
        m_next = jnp.maximum(m_prev, jnp.max(s, axis=-1, keepdims=True))
        m_ref[hh] = m_next
        alpha = jnp.tile(jnp.exp2(m_prev - m_next), (1, 2))
        p = jnp.exp2(s - jnp.tile(m_next, (1, t // LANES))).astype(BF16)
        v_aug = jnp.concatenate([v_ref[hh, 0, pl.ds(j * t, t), :], ones], axis=1)
        return alpha * acc_ref[hh] + jnp.dot(p, v_aug, preferred_element_type=F32)

    @pl.when(i == 0)
    def _():
        m_ref[...] = jnp.full(m_ref.shape, MASK_VALUE, F32)
        acc_ref[...] = jnp.zeros(acc_ref.shape, F32)
        for hh in range(hp):
            s_ref[hh] = scores(hh, 0)

    def body(j, carry):
        for hh in range(hp):
            acc = attend(hh, j, False)
            s_ref[hh] = scores(hh, j + 1)
            acc_ref[hh] = acc
        return carry

    lax.fori_loop(0, i, body, 0)

    lv = lam_ref[...]
    lam = (jnp.exp(jnp.sum(lv[0:1] * lv[1:2], axis=-1, keepdims=True))
           - jnp.exp(jnp.sum(lv[2:3] * lv[3:4], axis=-1, keepdims=True)) + lambda_init)
    for hh in range(hp):
        acc = attend(hh, i, True)
        s_ref[hh] = scores_of(stacked(qn_ref, hh), hh, 0)
        m_ref[hh] = jnp.full((2 * t, LANES), MASK_VALUE, F32)
        acc_ref[hh] = jnp.zeros((2 * t, 2 * DIFF_V_DIM), F32)
        attn = acc[:, :DIFF_V_DIM] / acc[:, DIFF_V_DIM:]
        o = attn[:t] - lam * attn[t:]
        o = _rms_norm(o, gsub_ref[...]) * (1.0 - lambda_init)
        o_ref[0, :, hh * DIFF_V_DIM:(hh + 1) * DIFF_V_DIM] = o.astype(BF16)


def _diff_attention(qkv, lam_vecs, g_sub, lambda_init, *, t=512, hp=4):
    _, b, s, _ = qkv.shape
    hg = DIFF_HEADS // hp
    return pl.pallas_call(
        functools.partial(_diff_attn_kernel, t=t, hp=hp, lambda_init=lambda_init),
        out_shape=jax.ShapeDtypeStruct((b, s, D_MODEL), BF16),
        grid=(b, hg, s // t),
        in_specs=[
            pl.BlockSpec((4, DIFF_SUB_DIM), lambda bi, hi, i: (0, 0)),
            pl.BlockSpec((1, DIFF_V_DIM), lambda bi, hi, i: (0, 0)),
            pl.BlockSpec((hp, 1, t, DIFF_V_DIM), lambda bi, hi, i: (hi, bi, i, 0)),
            pl.BlockSpec((hp, 1, t, DIFF_V_DIM),
                         lambda bi, hi, i: (hi, bi, jnp.minimum(i + 1, s // t - 1), 0)),
            pl.BlockSpec((hp, 1, s, DIFF_V_DIM), lambda bi, hi, i: (hg + hi, bi, 0, 0)),
            pl.BlockSpec((hp, 1, s, DIFF_V_DIM), lambda bi, hi, i: (2 * hg + hi, bi, 0, 0)),
        ],
        out_specs=pl.BlockSpec((1, t, hp * DIFF_V_DIM), lambda bi, hi, i: (bi, i, hi)),
        scratch_shapes=[pltpu.VMEM((hp, 2 * t, t), F32), pltpu.VMEM((hp, 2 * t, LANES), F32),
                        pltpu.VMEM((hp, 2 * t, 2 * DIFF_V_DIM), F32)],
        compiler_params=_params("arbitrary", "arbitrary", "arbitrary"),
        name="diff_attention",
    )(lam_vecs, g_sub.reshape(1, DIFF_V_DIM), qkv, qkv, qkv, qkv)


def _swa_kernel(sink_ref, q_ref, kp_ref, kc_ref, vp_ref, vc_ref, o_ref, bias_ref, *, tq, pair):
    i = pl.program_id(1)
    kk = jnp.concatenate([kp_ref[0], kc_ref[0]], axis=0)
    vv = jnp.concatenate([vp_ref[0], vc_ref[0]], axis=0)
    ones = jnp.ones((2 * BLOCK, LANES), BF16)
    rows = 2 * pair * BLOCK
    lane_q = lax.broadcasted_iota(jnp.int32, (BLOCK, LANES), 1)
    qi = lax.broadcasted_iota(jnp.int32, (BLOCK, 2 * BLOCK), 0)
    kj = lax.broadcasted_iota(jnp.int32, (BLOCK, 2 * BLOCK), 1)
    in_window = (kj > qi) & (kj <= qi + WINDOW)
    bias_ref[0] = jnp.where(in_window, 0.0, MASK_VALUE)
    bias_ref[1] = jnp.where(in_window & (kj >= jnp.where(i > 0, 0, BLOCK)), 0.0, MASK_VALUE)
    for c in range(tq // BLOCK):
        blk = slice(c * BLOCK, (c + 1) * BLOCK)
        k = kk[c * BLOCK:(c + 2) * BLOCK]
        v_aug = jnp.concatenate([vv[c * BLOCK:(c + 2) * BLOCK], ones], axis=1)
        for jc in range(D_MODEL // LANES // pair):
            parts = []
            for j in range(jc * pair, (jc + 1) * pair):
                qj = q_ref[0, blk, j * LANES:(j + 1) * LANES]
                parts.append(jnp.where(lane_q < SWA_HEAD_DIM, qj, jnp.zeros_like(qj)))
                parts.append(jnp.where(lane_q >= SWA_HEAD_DIM, qj, jnp.zeros_like(qj)))
            q2 = jnp.concatenate(parts, axis=0)
            s = lax.dot_general(q2, k, (((1,), (1,)), ((), ())), preferred_element_type=F32)
            s = s + jnp.tile(bias_ref[1 if c == 0 else 0], (rows // BLOCK, 1))
            sink = sink_ref[jc * rows:(jc + 1) * rows, :]
            m = jnp.maximum(jnp.max(s, axis=-1, keepdims=True), sink)
            p = jnp.exp2(s - jnp.tile(m, (1, 2))).astype(BF16)
            pv = jnp.dot(p, v_aug, preferred_element_type=F32)
            o2 = pv[:, :LANES] / (pv[:, LANES:] + jnp.exp2(sink - m))
            for jj in range(pair):
                j = jc * pair + jj
                lo = o2[(2 * jj) * BLOCK:(2 * jj + 1) * BLOCK]
                hi = o2[(2 * jj + 1) * BLOCK:(2 * jj + 2) * BLOCK]
                o_ref[0, blk, j * LANES:(j + 1) * LANES] = (
                    jnp.where(lane_q < SWA_HEAD_DIM, lo, hi).astype(BF16))


def _swa_attention(q, kv, sink_rows, *, tq=512, pair=2):
    b, s, _ = q.shape
    per = tq // BLOCK
    cur = lambda bi, i: (bi, i, 0)
    prev = lambda bi, i: (bi, jnp.maximum(i * per - 1, 0), 0)
    cur_v = lambda bi, i: (bi, i, 1)
    prev_v = lambda bi, i: (bi, jnp.maximum(i * per - 1, 0), 1)
    rows = sink_rows.shape[0]
    return pl.pallas_call(
        functools.partial(_swa_kernel, tq=tq, pair=pair),
        out_shape=jax.ShapeDtypeStruct((b, s, D_MODEL), BF16),
        grid=(b, s // tq),
        in_specs=[
            pl.BlockSpec((rows, LANES), lambda bi, i: (0, 0)),
            pl.BlockSpec((1, tq, D_MODEL), cur),
            pl.BlockSpec((1, BLOCK, LANES), prev),
            pl.BlockSpec((1, tq, LANES), cur),
            pl.BlockSpec((1, BLOCK, LANES), prev_v),
            pl.BlockSpec((1, tq, LANES), cur_v),
        ],
        out_specs=pl.BlockSpec((1, tq, D_MODEL), cur),
        scratch_shapes=[pltpu.VMEM((2, BLOCK, 2 * BLOCK), F32)],
        compiler_params=_params("arbitrary", "arbitrary"),
        name="swa_attention",
    )(sink_rows, q, kv, kv, kv, kv)


def _paired_head_order():
    cols = []
    for j in range(SWA_GROUP):
        cols.extend(range(j * SWA_HEAD_DIM, (j + 1) * SWA_HEAD_DIM))
        cols.extend(range((SWA_GROUP + j) * SWA_HEAD_DIM, (SWA_GROUP + j + 1) * SWA_HEAD_DIM))
    return jnp.asarray(cols, dtype=jnp.int32)


def kernel(x, positions, ln_ffn1, ffn1_w_gate, ffn1_w_up, ffn1_w_down, ln_mix, ln_ffn2, ffn2_w_gate, ffn2_w_up, ffn2_w_down, a_w_qkv, a_w_o, a_lambda_q1, a_lambda_k1, a_lambda_q2, a_lambda_k2, a_subln, b_w_q, b_b_q, b_sinks, b_w_o, b_b_o, kv_norm, kv_w_k, kv_b_k, kv_w_v, kv_b_v, final_norm):
    batch, seq, _ = x.shape
    n = batch * seq
    tables = _rope_tables(positions)
    h = x.reshape(n, D_MODEL)
    perm = _paired_head_order()
    kv = None
    for layer in range(DEPTH):
        if layer < N_A_LAYERS:
            a = layer
            proj = (ln_mix[layer], a_w_qkv[a].astype(BF16), jnp.zeros((3 * D_MODEL,), F32), tables,
                    dict(rope_cols=2 * D_MODEL, scale_cols=D_MODEL,
                         scale=DIFF_SUB_DIM ** -0.5 * LOG2_E, head_major=True))
        else:
            b = layer - N_A_LAYERS
            proj = (ln_mix[layer], b_w_q[b][:, perm].astype(BF16), b_b_q[b][perm], tables,
                    dict(rope_cols=D_MODEL, scale_cols=D_MODEL,
                         scale=SWA_HEAD_DIM ** -0.5 * LOG2_E, head_major=False))
        h, mixer_in = _ffn(h, ln_ffn1[layer], ffn1_w_gate[layer].astype(BF16),
                           ffn1_w_up[layer].astype(BF16), ffn1_w_down[layer].astype(BF16), proj=proj)
        if layer < N_A_LAYERS:
            lambda_init = 0.8 - 0.6 * math.exp(-0.3 * layer)
            lam_vecs = jnp.stack([a_lambda_q1[a], a_lambda_k1[a], a_lambda_q2[a], a_lambda_k2[a]])
            o = _diff_attention(mixer_in.reshape(3 * DIFF_HEADS, batch, seq, DIFF_V_DIM), lam_vecs,
                                a_subln[a], lambda_init)
            attn = (o.reshape(n, D_MODEL), a_w_o[a].astype(BF16), jnp.zeros((D_MODEL,), F32))
        else:
            sink_pairs = jnp.stack([b_sinks[b][:SWA_GROUP], b_sinks[b][SWA_GROUP:]], axis=1)
            sink_rows = jnp.broadcast_to(
                jnp.repeat(sink_pairs.reshape(-1) * LOG2_E, BLOCK)[:, None], (2 * D_MODEL, LANES))
            o = _swa_attention(mixer_in.reshape(batch, seq, D_MODEL), kv, sink_rows)
            attn = (o.reshape(n, D_MODEL), b_w_o[b][perm, :].astype(BF16), b_b_o[b])
        proj = None
        if layer == N_A_LAYERS - 1:
            proj = (kv_norm, jnp.concatenate([kv_w_k, kv_w_v], axis=1).astype(BF16),
                    jnp.concatenate([kv_b_k, kv_b_v]), tables,
                    dict(rope_cols=SWA_KV_WIDTH, scale_cols=0, scale=1.0, head_major=False))
        res = _ffn(h, ln_ffn2[layer], ffn2_w_gate[layer].astype(BF16), ffn2_w_up[layer].astype(BF16),
                   ffn2_w_down[layer].astype(BF16), attn=attn, proj=proj,
                   final_g=final_norm if layer == DEPTH - 1 else None)
        if proj is None:
            h = res
        else:
            h, kv = res[0], res[1].reshape(batch, seq, 2 * SWA_KV_WIDTH)
    return h.reshape(batch, seq, D_MODEL)
```
